```python
import jax, jax.numpy as jnp
from jax import lax
import numpy as np

D_MODEL = 1024
BATCH = 2
SEQ = 8192
DEPTH = 4
DEC_BATCH = 32
DEC_SEQ = 8
PAST_LEN = 8192
PAGE_SIZE = 128

N_MIXERS = 2
N_MOBA_LAYERS = (DEPTH + 1) // 2
N_MLSTM_LAYERS = DEPTH // 2
ATT_HEADS = 8
ATT_HEAD_DIM = D_MODEL // ATT_HEADS
MOBA_BLOCK = 256
MOBA_TOPK = 3
MOBA_QCHUNK = 64
MLSTM_HEADS = 8
MLSTM_DV = D_MODEL // MLSTM_HEADS
MLSTM_DK = MLSTM_DV // 2
MLSTM_CHUNK = 64
MLSTM_IN = 2 * MLSTM_HEADS * MLSTM_DK + 2 * MLSTM_HEADS * MLSTM_DV + 2 * MLSTM_HEADS
FORGET_BIAS = 3.0
D_FF = 4 * D_MODEL
RMS_EPS = 1e-6

kernel_name = 'moba_mlstm_hybrid_step'


def rmsnorm(x, g):
    xf = x.astype(jnp.float32)
    y = xf * lax.rsqrt(jnp.mean(xf * xf, axis=-1, keepdims=True) + RMS_EPS)
    return (y * g.astype(jnp.float32)).astype(x.dtype)


def sq_relu_mlp(x, w_up, w_down):
    return jnp.square(jax.nn.relu(x @ w_up)) @ w_down


def moba_project(x, w_qkv, q_gain, k_gain):
    b, s, _ = x.shape
    qkv = (x @ w_qkv).reshape(b, s, 3, ATT_HEADS, ATT_HEAD_DIM)
    q = rmsnorm(qkv[:, :, 0], q_gain)
    k = rmsnorm(qkv[:, :, 1], k_gain)
    v = qkv[:, :, 2]
    return q, k, v


def moba_blocks(k, v):
    b, l = k.shape[:2]
    pad = (-l) % MOBA_BLOCK
    if pad:
        k = jnp.pad(k, ((0, 0), (0, pad), (0, 0), (0, 0)))
        v = jnp.pad(v, ((0, 0), (0, pad), (0, 0), (0, 0)))
    nb = (l + pad) // MOBA_BLOCK
    kb = k.reshape(b, nb, MOBA_BLOCK, ATT_HEADS, ATT_HEAD_DIM)
    vb = v.reshape(b, nb, MOBA_BLOCK, ATT_HEADS, ATT_HEAD_DIM)
    k_mean = jnp.mean(kb.astype(jnp.float32), axis=2)
    return kb, vb, k_mean


def moba_queries(q, kb, vb, k_mean, q_pos):
    b, h, nq, dh = q.shape
    nb = kb.shape[1]
    topk = min(MOBA_TOPK, nb)
    own = q_pos // MOBA_BLOCK
    gate = jnp.einsum('bhqd,bnhd->bhqn', q, k_mean, preferred_element_type=jnp.float32)
    past = jnp.arange(nb, dtype=jnp.int32)[None, :] < own[:, None]
    gate = jnp.where(past, gate, -jnp.inf)
    _, sel = lax.top_k(gate, topk)
    own_b = jnp.broadcast_to(own[:, None], (b, h, nq, 1))
    idx = jnp.concatenate([sel, own_b], axis=-1)
    valid = jnp.concatenate([sel < own[:, None], jnp.ones((b, h, nq, 1), dtype=bool)], axis=-1)
    bi = jnp.arange(b)[:, None, None, None]
    hi = jnp.arange(h)[None, :, None, None]
    k_sel = kb[bi, idx, :, hi]
    v_sel = vb[bi, idx, :, hi]
    key_pos = idx[..., None] * MOBA_BLOCK + jnp.arange(MOBA_BLOCK, dtype=jnp.int32)
    mask = valid[..., None] & (key_pos <= q_pos[:, None, None])
    s = jnp.einsum('bhqd,bhqjkd->bhqjk', q, k_sel, preferred_element_type=jnp.float32) * (dh ** -0.5)
    s = jnp.where(mask, s, -jnp.inf).reshape(b, h, nq, -1)
    p = jax.nn.softmax(s, axis=-1).reshape(k_sel.shape[:-1]).astype(v_sel.dtype)
    return jnp.einsum('bhqjk,bhqjkd->bhqd', p, v_sel)


def moba_sweep(qh, kb, vb, k_mean, pos0, qchunk):
    b, h, nq, dh = qh.shape
    nc = nq // qchunk

    def one(c):
        start = c * qchunk
        qc = lax.dynamic_slice_in_dim(qh, start, qchunk, axis=2)
        pos = pos0 + start + jnp.arange(qchunk, dtype=jnp.int32)
        return moba_queries(qc, kb, vb, k_mean, pos)

    out = lax.map(one, jnp.arange(nc, dtype=jnp.int32))
    return out.transpose(1, 0, 3, 2, 4).reshape(b, nq, h * dh)


def moba_prompt(q, k, v):
    kb, vb, k_mean = moba_blocks(k, v)
    return moba_sweep(q.transpose(0, 2, 1, 3), kb, vb, k_mean, 0, MOBA_QCHUNK)


def moba_sample(q, k_new, v_new, cache_k, cache_v, page_table, layer):
    db, t = q.shape[:2]
    past_k = cache_k[layer, page_table].reshape(db, -1, ATT_HEADS, ATT_HEAD_DIM)
    past_v = cache_v[layer, page_table].reshape(db, -1, ATT_HEADS, ATT_HEAD_DIM)
    past_len = past_k.shape[1]
    pad = (-(past_len + t)) % MOBA_BLOCK
    zeros = jnp.zeros((db, pad, ATT_HEADS, ATT_HEAD_DIM), past_k.dtype)
    k_all = jnp.concatenate([past_k, k_new.astype(past_k.dtype), zeros], axis=1)
    v_all = jnp.concatenate([past_v, v_new.astype(past_v.dtype), zeros], axis=1)
    kb, vb, k_mean = moba_blocks(k_all, v_all)
    return moba_sweep(q.transpose(0, 2, 1, 3), kb, vb, k_mean, past_len, 1)


def mlstm_project(x, w_in, b_gates):
    b, s, _ = x.shape
    hk = MLSTM_HEADS * MLSTM_DK
    hv = MLSTM_HEADS * MLSTM_DV
    z = x @ w_in
    q = z[..., :hk].reshape(b, s, MLSTM_HEADS, MLSTM_DK)
    k = z[..., hk:2 * hk].reshape(b, s, MLSTM_HEADS, MLSTM_DK) * (MLSTM_DK ** -0.5)
    v = z[..., 2 * hk:2 * hk + hv].reshape(b, s, MLSTM_HEADS, MLSTM_DV)
    o = jax.nn.sigmoid(z[..., 2 * hk + hv:2 * hk + 2 * hv])
    g = z[..., 2 * hk + 2 * hv:].astype(jnp.float32) + b_gates.astype(jnp.float32)
    ig = g[..., :MLSTM_HEADS]
    fg_log = jax.nn.log_sigmoid(g[..., MLSTM_HEADS:])
    return q, k, v, o, ig, fg_log


def mlstm_chunkwise(q, k, v, ig, fg_log, C0, n0, m0, chunk):
    b, s = q.shape[:2]
    nc = s // chunk

    def split(a):
        a = a.astype(jnp.float32).reshape(b, nc, chunk, MLSTM_HEADS, -1)
        return a.transpose(1, 0, 3, 2, 4)

    xs = (split(q), split(k), split(v), split(ig[..., None])[..., 0], split(fg_log[..., None])[..., 0])
    causal = jnp.tril(jnp.ones((chunk, chunk), dtype=bool))

    def step(carry, inp):
        C, n, m = carry
        qc, kc, vc, ic, fc = inp
        bcum = jnp.cumsum(fc, axis=-1)
        m_t = bcum + jnp.maximum(m[..., None], lax.cummax(ic - bcum, axis=2))
        inter = jnp.exp(bcum + m[..., None] - m_t)
        log_d = bcum[..., :, None] - bcum[..., None, :] + ic[..., None, :] - m_t[..., :, None]
        dmat = jnp.exp(jnp.where(causal, log_d, -jnp.inf))
        w = dmat * jnp.einsum('bhtd,bhsd->bhts', qc, kc)
        num = inter[..., None] * jnp.einsum('bhvd,bhtd->bhtv', C, qc) + jnp.einsum('bhts,bhsv->bhtv', w, vc)
        den = inter * jnp.einsum('bhd,bhtd->bht', n, qc) + jnp.sum(w, axis=-1)
        h = num / jnp.maximum(jnp.abs(den), jnp.exp(-m_t))[..., None]
        m_new = m_t[..., -1]
        decay = jnp.exp(bcum[..., -1:] - bcum + ic - m_new[..., None])
        carry_scale = jnp.exp(bcum[..., -1] + m - m_new)
        C_new = carry_scale[..., None, None] * C + jnp.einsum('bhs,bhsv,bhsd->bhvd', decay, vc, kc)
        n_new = carry_scale[..., None] * n + jnp.einsum('bhs,bhsd->bhd', decay, kc)
        return (C_new, n_new, m_new), h

    init = (C0.astype(jnp.float32), n0.astype(jnp.float32), m0.astype(jnp.float32))
    (C, n, m), hs = lax.scan(step, init, xs)
    h = hs.transpose(1, 0, 3, 2, 4).reshape(b, s, MLSTM_HEADS, MLSTM_DV)
    return h, C, n, m


def mlstm_mix(x, w_in, b_gates, h_gain, w_out, C0, n0, m0, chunk):
    b, s, _ = x.shape
    q, k, v, o, ig, fg_log = mlstm_project(x, w_in, b_gates)
    h, C, n, m = mlstm_chunkwise(q, k, v, ig, fg_log, C0, n0, m0, chunk)
    h = rmsnorm(h, h_gain.reshape(MLSTM_HEADS, MLSTM_DV))
    y = (h.reshape(b, s, -1).astype(x.dtype) * o) @ w_out
    return y, C, n, m


def setup_inputs(seed: int = 0) -> dict:
    key = jax.random.key(seed)
    ks = jax.random.split(key, 24)
    f32 = jnp.float32
    n_pages = PAST_LEN // PAGE_SIZE
    n_used = DEC_BATCH * n_pages
    n_pool = n_used + n_used // 4

    def nrm(k, shape, scale=1.0):
        return scale * jax.random.normal(k, shape, f32)

    x_prompt = nrm(ks[0], (BATCH, SEQ, D_MODEL))
    x_sample = nrm(ks[1], (DEC_BATCH, DEC_SEQ, D_MODEL))
    cache_k = nrm(ks[2], (N_MOBA_LAYERS, n_pool, PAGE_SIZE, ATT_HEADS, ATT_HEAD_DIM))
    cache_v = nrm(ks[3], (N_MOBA_LAYERS, n_pool, PAGE_SIZE, ATT_HEADS, ATT_HEAD_DIM))
    state_C = nrm(ks[4], (N_MLSTM_LAYERS, DEC_BATCH, MLSTM_HEADS, MLSTM_DV, MLSTM_DK), 0.1)
    state_n = nrm(ks[5], (N_MLSTM_LAYERS, DEC_BATCH, MLSTM_HEADS, MLSTM_DK), 0.1)
    state_m = nrm(ks[6], (N_MLSTM_LAYERS, DEC_BATCH, MLSTM_HEADS))
    page_table = jax.random.permutation(ks[7], n_pool)[:n_used].reshape(DEC_BATCH, n_pages).astype(jnp.int32)

    norm_mix = 1.0 + nrm(ks[8], (DEPTH, D_MODEL), 0.02)
    norm_mlp = 1.0 + nrm(ks[9], (DEPTH, D_MODEL), 0.02)
    moba_w_qkv = nrm(ks[10], (N_MOBA_LAYERS, D_MODEL, 3 * ATT_HEADS * ATT_HEAD_DIM), D_MODEL ** -0.5)
    moba_q_gain = 1.0 + nrm(ks[11], (N_MOBA_LAYERS, ATT_HEAD_DIM), 0.02)
    moba_k_gain = 1.0 + nrm(ks[12], (N_MOBA_LAYERS, ATT_HEAD_DIM), 0.02)
    moba_w_out = nrm(ks[13], (N_MOBA_LAYERS, ATT_HEADS * ATT_HEAD_DIM, D_MODEL), (ATT_HEADS * ATT_HEAD_DIM) ** -0.5)
    mlstm_w_in = nrm(ks[14], (N_MLSTM_LAYERS, D_MODEL, MLSTM_IN), D_MODEL ** -0.5)
    gate_offset = jnp.concatenate([jnp.zeros((MLSTM_HEADS,), f32), jnp.full((MLSTM_HEADS,), FORGET_BIAS, f32)])
    mlstm_b_gates = gate_offset + nrm(ks[15], (N_MLSTM_LAYERS, 2 * MLSTM_HEADS), 0.1)
    mlstm_h_gain = 1.0 + nrm(ks[16], (N_MLSTM_LAYERS, MLSTM_HEADS * MLSTM_DV), 0.02)
    mlstm_w_out = nrm(ks[17], (N_MLSTM_LAYERS, MLSTM_HEADS * MLSTM_DV, D_MODEL), (MLSTM_HEADS * MLSTM_DV) ** -0.5)
    mlp_w_up = nrm(ks[18], (DEPTH, D_MODEL, D_FF), D_MODEL ** -0.5)
    mlp_w_down = nrm(ks[19], (DEPTH, D_FF, D_MODEL), D_FF ** -0.5)
    return {'x_prompt': x_prompt, 'x_sample': x_sample, 'cache_k': cache_k, 'cache_v': cache_v,
            'state_C': state_C, 'state_n': state_n, 'state_m': state_m, 'page_table': page_table,
            'norm_mix': norm_mix, 'norm_mlp': norm_mlp, 'moba_w_qkv': moba_w_qkv,
            'moba_q_gain': moba_q_gain, 'moba_k_gain': moba_k_gain, 'moba_w_out': moba_w_out,
            'mlstm_w_in': mlstm_w_in, 'mlstm_b_gates': mlstm_b_gates, 'mlstm_h_gain': mlstm_h_gain,
            'mlstm_w_out': mlstm_w_out, 'mlp_w_up': mlp_w_up, 'mlp_w_down': mlp_w_down}


def reference(x_prompt, x_sample, cache_k, cache_v, state_C, state_n, state_m, page_table,
              norm_mix, norm_mlp, moba_w_qkv, moba_q_gain, moba_k_gain, moba_w_out,
              mlstm_w_in, mlstm_b_gates, mlstm_h_gain, mlstm_w_out, mlp_w_up, mlp_w_down):
    xp, xs = x_prompt, x_sample
    k_p, v_p, k_s, v_s = [], [], [], []
    c_p, n_p, m_p, c_s, n_s, m_s = [], [], [], [], [], []
    for i in range(DEPTH):
        l = i // N_MIXERS
        hp = rmsnorm(xp, norm_mix[i])
        hs = rmsnorm(xs, norm_mix[i])
        if i % N_MIXERS == 0:
            qp, kp, vp = moba_project(hp, moba_w_qkv[l], moba_q_gain[l], moba_k_gain[l])
            yp = moba_prompt(qp, kp, vp) @ moba_w_out[l]
            qs, ks_, vs_ = moba_project(hs, moba_w_qkv[l], moba_q_gain[l], moba_k_gain[l])
            ys = moba_sample(qs, ks_, vs_, cache_k, cache_v, page_table, l) @ moba_w_out[l]
            k_p.append(kp); v_p.append(vp); k_s.append(ks_); v_s.append(vs_)
        else:
            b = xp.shape[0]
            zc = jnp.zeros((b, MLSTM_HEADS, MLSTM_DV, MLSTM_DK), jnp.float32)
            zn = jnp.zeros((b, MLSTM_HEADS, MLSTM_DK), jnp.float32)
            zm = jnp.zeros((b, MLSTM_HEADS), jnp.float32)
            yp, cp, np_, mp = mlstm_mix(hp, mlstm_w_in[l], mlstm_b_gates[l], mlstm_h_gain[l], mlstm_w_out[l],
                                        zc, zn, zm, min(MLSTM_CHUNK, xp.shape[1]))
            ys, cs, ns, ms = mlstm_mix(hs, mlstm_w_in[l], mlstm_b_gates[l], mlstm_h_gain[l], mlstm_w_out[l],
                                       state_C[l], state_n[l], state_m[l], xs.shape[1])
            c_p.append(cp); n_p.append(np_); m_p.append(mp)
            c_s.append(cs); n_s.append(ns); m_s.append(ms)
        xp = xp + yp.astype(xp.dtype)
        xs = xs + ys.astype(xs.dtype)
        xp = xp + sq_relu_mlp(rmsnorm(xp, norm_mlp[i]), mlp_w_up[i], mlp_w_down[i])
        xs = xs + sq_relu_mlp(rmsnorm(xs, norm_mlp[i]), mlp_w_up[i], mlp_w_down[i])
    k_prompt = jnp.stack(k_p)
    v_prompt = jnp.stack(v_p)
    C_prompt = jnp.stack(c_p)
    n_prompt = jnp.stack(n_p)
    m_prompt = jnp.stack(m_p)
    k_sample = jnp.stack(k_s)
    v_sample = jnp.stack(v_s)
    C_sample = jnp.stack(c_s)
    n_sample = jnp.stack(n_s)
    m_sample = jnp.stack(m_s)
    return (xp, xs, k_prompt, v_prompt, C_prompt, n_prompt, m_prompt, k_sample, v_sample, C_sample, n_sample, m_sample)
```

```python
import functools

import jax
import jax.numpy as jnp
from jax import lax
from jax.experimental import pallas as pl
from jax.experimental.pallas import tpu as pltpu

F32 = jnp.float32
BF16 = jnp.bfloat16
HIGHEST = lax.Precision.HIGHEST

RMS_EPS = 1e-6
MOBA_BLOCK = 256
MOBA_TOPK = 3
LANES = 128
SAMPLE_CHUNK = 128
PROMPT_CHUNK = 256
NEG = -1e30
VMEM_LIMIT = 56 * 1024 * 1024


def _params(*sem):
    return pltpu.CompilerParams(dimension_semantics=sem, vmem_limit_bytes=VMEM_LIMIT)


def _dot(a, b, precision=None):
    return jnp.dot(a, b, preferred_element_type=F32, precision=precision)


def _dot_nt(a, b, precision=None):
    return lax.dot_general(a, b, (((1,), (1,)), ((), ())), preferred_element_type=F32, precision=precision)


def _dot_tn(a, b):
    return lax.dot_general(a, b, (((0,), (0,)), ((), ())), preferred_element_type=F32)


def _rms(x, g):
    return x * lax.rsqrt(jnp.mean(x * x, axis=-1, keepdims=True) + RMS_EPS) * g


def _log_sigmoid(x):
    return -(jnp.maximum(-x, 0.0) + jnp.log1p(jnp.exp(-jnp.abs(x))))


def _resident(shape):
    return pl.BlockSpec(shape, lambda *_: (0,) * len(shape), pipeline_mode=pl.Buffered(1))


def _moba_proj_kernel(x_ref, g_ref, w_ref, qg_ref, kg_ref,
                      q_ref, k_ref, v_ref, kb_ref, vb_ref, km_ref=None, *, n_heads, dh):
    d = n_heads * dh
    xn = _rms(x_ref[...], g_ref[...])
    qkv = _dot(xn.astype(BF16), w_ref[...])
    for h in range(n_heads):
        sl = slice(h * dh, (h + 1) * dh)
        q_ref[:, sl] = _rms(qkv[:, h * dh:(h + 1) * dh], qg_ref[...])
        kn = _rms(qkv[:, d + h * dh:d + (h + 1) * dh], kg_ref[...])
        k_ref[:, sl] = kn
        kb_ref[:, sl] = kn.astype(BF16)
    v = qkv[:, 2 * d:]
    v_ref[...] = v
    vb_ref[...] = v.astype(BF16)
    if km_ref is not None:
        for r in range(km_ref.shape[0]):
            km_ref[r] = jnp.mean(k_ref[r * MOBA_BLOCK:(r + 1) * MOBA_BLOCK, :], axis=0, keepdims=True)


def _moba_proj(x, gain, w_bf, q_gain, k_gain, n_heads, tm, block_means):
    t, dm = x.shape
    d = w_bf.shape[1] // 3
    dh = d // n_heads
    row = lambda i: (i, 0)
    out_specs = [pl.BlockSpec((tm, d), row)] * 5
    out_shape = [jax.ShapeDtypeStruct((t, d), F32)] * 3 + [jax.ShapeDtypeStruct((t, d), BF16)] * 2
    if block_means:
        out_specs.append(pl.BlockSpec((tm // MOBA_BLOCK, 1, d), lambda i: (i, 0, 0)))
        out_shape.append(jax.ShapeDtypeStruct((t // MOBA_BLOCK, 1, d), F32))
    return pl.pallas_call(
        functools.partial(_moba_proj_kernel, n_heads=n_heads, dh=dh),
        grid=(t // tm,),
        in_specs=[pl.BlockSpec((tm, dm), row), _resident((1, dm)), _resident((dm, 3 * d)),
                  _resident((1, dh)), _resident((1, dh))],
        out_specs=out_specs,
        out_shape=out_shape,
        compiler_params=_params("parallel"),
        name="moba_proj",
    )(x, gain.reshape(1, dm), w_bf, q_gain.reshape(1, dh), k_gain.reshape(1, dh))


def _moba_attn_kernel(q_ref, k_ref, v_ref, km_ref, o_ref, *, scale):
    blk = q_ref.shape[0]
    i = pl.program_id(2)
    q = q_ref[...]
    gates = _dot_nt(q, km_ref[...], HIGHEST)
    lane = lax.broadcasted_iota(jnp.int32, gates.shape, 1)
    lanef = lane.astype(F32)
    g = jnp.where(lane < i, gates, -jnp.inf)
    selb = jnp.full(gates.shape, NEG, F32)
    for _ in range(MOBA_TOPK):
        m = jnp.max(g, axis=1, keepdims=True)
        idx = jnp.min(jnp.where(g == m, lanef, 1e9), axis=1, keepdims=True)
        pick = lanef == jnp.where(m > -jnp.inf, idx, -1.0)
        selb = jnp.where(pick, 0.0, selb)
        g = jnp.where(pick, -jnp.inf, g)

    qb = q.astype(BF16)
    rowi = lax.broadcasted_iota(jnp.int32, (blk, blk), 0)
    coli = lax.broadcasted_iota(jnp.int32, (blk, blk), 1)
    start = pl.multiple_of(i * blk, blk)
    s = _dot_nt(qb, k_ref[pl.ds(start, blk), :]) * scale
    s = jnp.where(coli <= rowi, s, -jnp.inf)
    m0 = jnp.max(s, axis=1, keepdims=True)
    p = jnp.exp(s - m0)
    l0 = jnp.sum(p, axis=1, keepdims=True)
    acc0 = _dot(p.astype(BF16), v_ref[pl.ds(start, blk), :])

    def body(j, carry):
        m, l, acc = carry
        st = pl.multiple_of(j * blk, blk)
        bias = jnp.sum(jnp.where(lane == j, selb, 0.0), axis=1, keepdims=True)
        s = _dot_nt(qb, k_ref[pl.ds(st, blk), :]) * scale + bias
        mn = jnp.maximum(m, jnp.max(s, axis=1, keepdims=True))
        alpha = jnp.exp(m - mn)
        p = jnp.exp(s - mn)
        l = alpha * l + jnp.sum(p, axis=1, keepdims=True)
        acc = alpha * acc + _dot(p.astype(BF16), v_ref[pl.ds(st, blk), :])
        return mn, l, acc

    _, l, acc = lax.fori_loop(0, i, body, (m0, l0, acc0))
    o_ref[...] = (acc / l).astype(o_ref.dtype)


def _moba_attn(q, k_bf, v_bf, k_mean, batch, seq, n_heads):
    d = q.shape[1]
    dh = d // n_heads
    nb = seq // MOBA_BLOCK
    return pl.pallas_call(
        functools.partial(_moba_attn_kernel, scale=dh ** -0.5),
        grid=(batch, n_heads, nb),
        in_specs=[pl.BlockSpec((MOBA_BLOCK, dh), lambda b, h, i: (b * nb + i, h)),
                  pl.BlockSpec((seq, dh), lambda b, h, i: (b, h)),
                  pl.BlockSpec((seq, dh), lambda b, h, i: (b, h)),
                  pl.BlockSpec((LANES, dh), lambda b, h, i: (b, h))],
        out_specs=pl.BlockSpec((MOBA_BLOCK, dh), lambda b, h, i: (b * nb + i, h)),
        out_shape=jax.ShapeDtypeStruct((batch * seq, d), BF16),
        compiler_params=_params("parallel", "parallel", "arbitrary"),
        name="moba_attn",
    )(q, k_bf, v_bf, k_mean)


def _moba_sample_kernel(pt_ref, q_ref, kn_ref, vn_ref, k0_ref, k1_ref, v0_ref, v1_ref, o_ref,
                        qbd_ref, qbf_ref, g_all, m_all, l_all, o_all, *, n_heads, dh, scale):
    del pt_ref
    j = pl.program_id(1)
    nblk = pl.num_programs(1)
    t = q_ref.shape[0]
    r = n_heads * t
    page = k0_ref.shape[0]

    @pl.when(j == 0)
    def _():
        q = q_ref[...]
        col_head = lax.broadcasted_iota(jnp.int32, q.shape, 1) // dh
        for h in range(n_heads):
            qbd_ref[h * t:(h + 1) * t, :] = jnp.where(col_head == h, q, 0.0)
        qbf_ref[...] = qbd_ref[...].astype(BF16)

    qbf = qbf_ref[...]

    def head_diag(of):
        return jnp.concatenate([of[h * t:(h + 1) * t, h * dh:(h + 1) * dh] for h in range(n_heads)], axis=0)

    k0 = k0_ref[...]
    k1 = k1_ref[...]
    s0 = _dot_nt(qbf, k0.astype(BF16)) * scale
    s1 = _dot_nt(qbf, k1.astype(BF16)) * scale
    m = jnp.maximum(jnp.max(s0, axis=1, keepdims=True), jnp.max(s1, axis=1, keepdims=True))
    p0 = jnp.exp(s0 - m)
    p1 = jnp.exp(s1 - m)
    l = jnp.sum(p0, axis=1, keepdims=True) + jnp.sum(p1, axis=1, keepdims=True)
    of = _dot(p0.astype(BF16), v0_ref[...].astype(BF16)) + _dot(p1.astype(BF16), v1_ref[...].astype(BF16))
    k_mean = (jnp.sum(k0, axis=0, keepdims=True) + jnp.sum(k1, axis=0, keepdims=True)) * (1.0 / (2 * page))
    gate = jnp.sum(qbd_ref[...] * k_mean, axis=1, keepdims=True)
    g_all[j] = jnp.broadcast_to(gate, (r, LANES))
    m_all[j] = jnp.broadcast_to(m, (r, LANES))
    l_all[j] = jnp.broadcast_to(l, (r, LANES))
    o_all[j] = head_diag(of)

    @pl.when(j == nblk - 1)
    def _():
        pad = jnp.zeros((page - t, kn_ref.shape[1]), F32)
        kp = jnp.concatenate([kn_ref[...], pad], axis=0).astype(BF16)
        vp = jnp.concatenate([vn_ref[...], pad], axis=0).astype(BF16)
        so = _dot_nt(qbf, kp) * scale
        tok = lax.rem(lax.broadcasted_iota(jnp.int32, so.shape, 0), t)
        key = lax.broadcasted_iota(jnp.int32, so.shape, 1)
        so = jnp.where(key <= tok, so, -jnp.inf)
        m_own = jnp.max(so, axis=1, keepdims=True)
        po = jnp.exp(so - m_own)
        l_own = jnp.sum(po, axis=1, keepdims=True)
        o_own = head_diag(_dot(po.astype(BF16), vp))

        g = g_all[...]
        jj = lax.broadcasted_iota(jnp.int32, g.shape, 0).astype(F32)
        sel = jnp.zeros(g.shape, F32)
        for _ in range(MOBA_TOPK):
            gm = jnp.max(g, axis=0, keepdims=True)
            idx = jnp.min(jnp.where(g == gm, jj, 1e9), axis=0, keepdims=True)
            pick = jj == jnp.where(gm > -jnp.inf, idx, -1.0)
            sel = jnp.where(pick, 1.0, sel)
            g = jnp.where(pick, -jnp.inf, g)
        ma = m_all[...]
        mm = jnp.maximum(jnp.max(jnp.where(sel > 0, ma, -jnp.inf), axis=0), m_own)
        w = jnp.where(sel > 0, jnp.exp(ma - mm), 0.0)
        w_own = jnp.exp(m_own - mm)
        den = jnp.sum(w * l_all[...], axis=0) + w_own * l_own
        num = jnp.sum(w * o_all[...], axis=0) + w_own * o_own
        res = num / den
        for h in range(n_heads):
            o_ref[:, h * dh:(h + 1) * dh] = res[h * t:(h + 1) * t, :]


def _moba_sample(q, k_new, v_new, cache_k, cache_v, page_table, layer, n_seq, n_heads):
    d = q.shape[1]
    dh = d // n_heads
    t = q.shape[0] // n_seq
    page = cache_k.shape[2]
    assert MOBA_BLOCK == 2 * page and t <= page
    nblk = page_table.shape[1] // 2
    assert nblk >= MOBA_TOPK
    r = n_heads * t
    tok = pl.BlockSpec((t, d), lambda s, j, pt: (s, 0))
    pg = lambda o: pl.BlockSpec((None, None, page, d), lambda s, j, pt: (layer, pt[s, 2 * j + o], 0, 0))
    stat = pltpu.VMEM((nblk, r, LANES), F32)
    return pl.pallas_call(
        functools.partial(_moba_sample_kernel, n_heads=n_heads, dh=dh, scale=dh ** -0.5),
        grid_spec=pltpu.PrefetchScalarGridSpec(
            num_scalar_prefetch=1,
            grid=(n_seq, nblk),
            in_specs=[tok, tok, tok, pg(0), pg(1), pg(0), pg(1)],
            out_specs=tok,
            scratch_shapes=[pltpu.VMEM((r, d), F32), pltpu.VMEM((r, d), BF16), stat, stat, stat, stat]),
        out_shape=jax.ShapeDtypeStruct(q.shape, F32),
        compiler_params=_params("parallel", "arbitrary"),
        name="moba_sample",
    )(page_table, q, k_new, v_new, cache_k, cache_k, cache_v, cache_v)


def _mlstm_proj_kernel(x_ref, g_ref, w_ref, wg_ref, wgt_ref, bg_ref, bgt_ref,
                       q_ref, k_ref, v_ref, o_ref, gt_ref, gtt_ref, *, n_heads, hk, hv):
    xn = _rms(x_ref[...], g_ref[...])
    z = _dot(xn.astype(BF16), w_ref[...])
    dk = hk // n_heads
    q_ref[...] = z[:, :hk].astype(BF16)
    k_ref[...] = (z[:, hk:2 * hk] * dk ** -0.5).astype(BF16)
    v_ref[...] = z[:, 2 * hk:2 * hk + hv].astype(BF16)
    o_ref[...] = jax.nn.sigmoid(z[:, 2 * hk + hv:])
    g = _dot(xn, wg_ref[...], HIGHEST) + bg_ref[...]
    c = lax.broadcasted_iota(jnp.int32, g.shape, 1)
    gt_ref[...] = jnp.where((c >= n_heads) & (c < 2 * n_heads), _log_sigmoid(g), g)
    gt = _dot_nt(wgt_ref[...], xn, HIGHEST) + bgt_ref[...]
    rr = lax.broadcasted_iota(jnp.int32, gt.shape, 0)
    gtt_ref[...] = jnp.where((rr >= n_heads) & (rr < 2 * n_heads), _log_sigmoid(gt), gt)


def _mlstm_proj(x, gain, w_in, b_gates, n_heads, tm):
    t, dm = x.shape
    ng = 2 * n_heads
    hv = dm
    hk = (w_in.shape[1] - ng - 2 * hv) // 2
    w_main = w_in[:, :2 * hk + 2 * hv].astype(BF16)
    w_g = jnp.pad(w_in[:, 2 * hk + 2 * hv:], ((0, 0), (0, LANES - ng)))
    b_g = jnp.pad(b_gates, (0, LANES - ng))
    row = lambda i: (i, 0)
    return pl.pallas_call(
        functools.partial(_mlstm_proj_kernel, n_heads=n_heads, hk=hk, hv=hv),
        grid=(t // tm,),
        in_specs=[pl.BlockSpec((tm, dm), row), _resident((1, dm)), _resident((dm, 2 * hk + 2 * hv)),
                  _resident((dm, LANES)), _resident((LANES, dm)), _resident((1, LANES)), _resident((LANES, tm))],
        out_specs=[pl.BlockSpec((tm, hk), row), pl.BlockSpec((tm, hk), row), pl.BlockSpec((tm, hv), row),
                   pl.BlockSpec((tm, hv), row), pl.BlockSpec((tm, LANES), row),
                   pl.BlockSpec((LANES, tm), lambda i: (0, i))],
        out_shape=[jax.ShapeDtypeStruct((t, hk), BF16), jax.ShapeDtypeStruct((t, hk), BF16),
                   jax.ShapeDtypeStruct((t, hv), BF16), jax.ShapeDtypeStruct((t, hv), F32),
                   jax.ShapeDtypeStruct((t, LANES), F32), jax.ShapeDtypeStruct((LANES, t), F32)],
        compiler_params=_params("parallel"),
        name="mlstm_proj",
    )(x, gain.reshape(1, dm), w_main, w_g, w_g.T, b_g.reshape(1, LANES),
      jnp.broadcast_to(b_g[:, None], (LANES, tm)))


def _mlstm_kernel(q_ref, k_ref, v_ref, o_ref, g_ref, gt_ref, hg_ref, c0_ref, n0_ref, m0_ref,
                  y_ref, c_ref, n_ref, m_ref, *, n_heads):
    @pl.when(pl.program_id(1) == 0)
    def _():
        c_ref[...] = c0_ref[...]
        n_ref[...] = n0_ref[...]
        m_ref[...] = m0_ref[...]

    ln = q_ref.shape[0]
    dk = q_ref.shape[1] // n_heads
    dv = v_ref.shape[1] // n_heads
    ti = lax.broadcasted_iota(jnp.int32, (ln, ln), 0)
    si = lax.broadcasted_iota(jnp.int32, (ln, ln), 1)
    tri = si <= ti
    g = g_ref[...]
    gt = gt_ref[...]
    m_row = m_ref[0]
    lane = lax.broadcasted_iota(jnp.int32, m_row.shape, 1)
    for h in range(n_heads):
        ig_row, fg_row = gt[h:h + 1, :], gt[n_heads + h:n_heads + h + 1, :]
        ig_col, fg_col = g[:, h:h + 1], g[:, n_heads + h:n_heads + h + 1]
        bcum_col = jnp.sum(jnp.where(tri, fg_row, 0.0), axis=1, keepdims=True)
        bcum_row = jnp.sum(jnp.where(ti <= si, fg_col, 0.0), axis=0, keepdims=True)
        a_row = ig_row - bcum_row
        a_col = ig_col - bcum_col
        m_prev = m_row[:, h:h + 1]
        mx = jnp.maximum(m_prev, jnp.max(jnp.where(tri, a_row, -jnp.inf), axis=1, keepdims=True))
        dmat = jnp.exp(jnp.where(tri, a_row - mx, -jnp.inf))
        inter = jnp.exp(m_prev - mx)
        qh = q_ref[:, h * dk:(h + 1) * dk]
        kh = k_ref[:, h * dk:(h + 1) * dk]
        vh = v_ref[:, h * dv:(h + 1) * dv]
        w = dmat * _dot_nt(qh, kh)
        c_prev = c_ref[0, h]
        n_prev = n_ref[0, h:h + 1, :]
        num = inter * _dot_nt(qh, c_prev.astype(BF16)) + _dot(w.astype(BF16), vh)
        den = (inter * jnp.sum(qh.astype(F32) * n_prev, axis=1, keepdims=True)
               + jnp.sum(w, axis=1, keepdims=True))
        hh = num / jnp.maximum(jnp.abs(den), jnp.exp(-(bcum_col + mx)))
        mx_last = mx[ln - 1:ln, :]
        decay = jnp.exp(a_col - mx_last)
        carry = jnp.exp(m_prev - mx_last)
        c_ref[0, h] = carry * c_prev + _dot_tn((decay * vh.astype(F32)).astype(BF16), kh)
        n_ref[0, h:h + 1, :] = carry * n_prev + jnp.sum(decay * kh.astype(F32), axis=0, keepdims=True)
        m_row = jnp.where(lane == h, bcum_col[ln - 1:ln, :] + mx_last, m_row)
        hn = _rms(hh, hg_ref[:, h * dv:(h + 1) * dv])
        y_ref[:, h * dv:(h + 1) * dv] = (hn * o_ref[:, h * dv:(h + 1) * dv]).astype(y_ref.dtype)
    m_ref[0] = m_row


def _mlstm_scan(q, k, v, o, gates, gates_t, h_gain, c0, n0, m0, n_seq, chunk, n_heads):
    t, hk = q.shape
    hv = v.shape[1]
    nc = t // n_seq // chunk
    row = lambda s, c: (s * nc + c, 0)
    st4 = pl.BlockSpec((1,) + c0.shape[1:], lambda s, c: (s, 0, 0, 0))
    st3n = pl.BlockSpec((1,) + n0.shape[1:], lambda s, c: (s, 0, 0))
    st3m = pl.BlockSpec((1, 1, LANES), lambda s, c: (s, 0, 0))
    return pl.pallas_call(
        functools.partial(_mlstm_kernel, n_heads=n_heads),
        grid=(n_seq, nc),
        in_specs=[pl.BlockSpec((chunk, hk), row), pl.BlockSpec((chunk, hk), row), pl.BlockSpec((chunk, hv), row),
                  pl.BlockSpec((chunk, hv), row), pl.BlockSpec((chunk, LANES), row),
                  pl.BlockSpec((LANES, chunk), lambda s, c: (0, s * nc + c)),
                  pl.BlockSpec((1, hv), lambda s, c: (0, 0)), st4, st3n, st3m],
        out_specs=[pl.BlockSpec((chunk, hv), row), st4, st3n, st3m],
        out_shape=[jax.ShapeDtypeStruct((t, hv), BF16), jax.ShapeDtypeStruct(c0.shape, F32),
                   jax.ShapeDtypeStruct(n0.shape, F32), jax.ShapeDtypeStruct(m0.shape, F32)],
        compiler_params=_params("parallel", "arbitrary"),
        name="mlstm_scan",
    )(q, k, v, o, gates, gates_t, h_gain.reshape(1, hv), c0, n0, m0)


def _pad_rows(a, n_seq, chunk, fill=0.0):
    t = a.shape[0] // n_seq
    a = a.reshape(n_seq, t, a.shape[1])
    a = jnp.pad(a, ((0, 0), (0, chunk - t), (0, 0)), constant_values=fill)
    return a.reshape(n_seq * chunk, a.shape[2])


def _mlp_kernel(x_ref, a_ref, wo_ref, g_ref, wu_ref, wd_ref, out_ref, *, ff_chunk):
    x1 = x_ref[...] + _dot(a_ref[...].astype(BF16), wo_ref[...])
    xn = _rms(x1, g_ref[...]).astype(BF16)
    acc = x1
    for c in range(wu_ref.shape[1] // ff_chunk):
        hid = _dot(xn, wu_ref[:, c * ff_chunk:(c + 1) * ff_chunk])
        hid = jnp.square(jnp.maximum(hid, 0.0)).astype(BF16)
        acc = acc + _dot(hid, wd_ref[c * ff_chunk:(c + 1) * ff_chunk, :])
    out_ref[...] = acc


def _mix_out_mlp(x, a, w_out_bf, gain, w_up_bf, w_down_bf, tm):
    t, dm = x.shape
    dff = w_up_bf.shape[1]
    row = lambda i: (i, 0)
    return pl.pallas_call(
        functools.partial(_mlp_kernel, ff_chunk=1024),
        grid=(t // tm,),
        in_specs=[pl.BlockSpec((tm, dm), row), pl.BlockSpec((tm, a.shape[1]), row), _resident(w_out_bf.shape),
                  _resident((1, dm)), _resident((dm, dff)), _resident((dff, dm))],
        out_specs=pl.BlockSpec((tm, dm), row),
        out_shape=jax.ShapeDtypeStruct((t, dm), F32),
        compiler_params=_params("parallel"),
        name="mix_out_mlp",
    )(x, a, w_out_bf, gain.reshape(1, dm), w_up_bf, w_down_bf)


def kernel(x_prompt, x_sample, cache_k, cache_v, state_C, state_n, state_m, page_table, norm_mix, norm_mlp,
           moba_w_qkv, moba_q_gain, moba_k_gain, moba_w_out, mlstm_w_in, mlstm_b_gates, mlstm_h_gain,
           mlstm_w_out, mlp_w_up, mlp_w_down):
    batch, seq, dm = x_prompt.shape
    n_dec, t_dec, _ = x_sample.shape
    depth = norm_mix.shape[0]
    att_heads, att_dh = cache_k.shape[3], cache_k.shape[4]
    ml_heads, ml_dv, ml_dk = state_C.shape[2], state_C.shape[3], state_C.shape[4]
    n_prompt_rows, n_sample_rows = batch * seq, n_dec * t_dec
    tm_p = 512
    nb = seq // MOBA_BLOCK
    assert seq % max(MOBA_BLOCK, PROMPT_CHUNK, tm_p) == 0 and nb <= LANES and t_dec <= SAMPLE_CHUNK

    xp = x_prompt.reshape(n_prompt_rows, dm)
    xs = x_sample.reshape(n_sample_rows, dm)
    ck = cache_k.reshape(cache_k.shape[:3] + (att_heads * att_dh,))
    cv = cache_v.reshape(cache_v.shape[:3] + (att_heads * att_dh,))
    outs = {name: [] for name in ("kp", "vp", "cp", "np", "mp", "ks", "vs", "cs", "ns", "ms")}

    for i in range(depth):
        l = i // 2
        if i % 2 == 0:
            w_qkv = moba_w_qkv[l].astype(BF16)
            qp, kp, vp, kp_bf, vp_bf, kmean = _moba_proj(xp, norm_mix[i], w_qkv, moba_q_gain[l], moba_k_gain[l],
                                                         att_heads, tm_p, True)
            kmean = jnp.pad(kmean.reshape(batch, nb, -1), ((0, 0), (0, LANES - nb), (0, 0)))
            ap = _moba_attn(qp, kp_bf, vp_bf, kmean.reshape(batch * LANES, -1), batch, seq, att_heads)
            qs, ks, vs, _, _ = _moba_proj(xs, norm_mix[i], w_qkv, moba_q_gain[l], moba_k_gain[l],
                                          att_heads, n_sample_rows, False)
            a_s = _moba_sample(qs, ks, vs, ck, cv, page_table, l, n_dec, att_heads)
            outs["kp"].append(kp); outs["vp"].append(vp); outs["ks"].append(ks); outs["vs"].append(vs)
            w_out = moba_w_out[l].astype(BF16)
        else:
            zc = jnp.zeros((batch, ml_heads, ml_dv, ml_dk), F32)
            zn = jnp.zeros((batch, ml_heads, ml_dk), F32)
            zm = jnp.zeros((batch, 1, LANES), F32)
            q, k, v, o, g, gt = _mlstm_proj(xp, norm_mix[i], mlstm_w_in[l], mlstm_b_gates[l], ml_heads, tm_p)
            ap, c_p, n_p, m_p = _mlstm_scan(q, k, v, o, g, gt, mlstm_h_gain[l], zc, zn, zm,
                                            batch, PROMPT_CHUNK, ml_heads)
            q, k, v, o, g, gt = _mlstm_proj(xs, norm_mix[i], mlstm_w_in[l], mlstm_b_gates[l], ml_heads,
                                            n_sample_rows)
            pad = lambda a: _pad_rows(a, n_dec, SAMPLE_CHUNK)
            idle = jnp.where(jnp.arange(LANES) < ml_heads, -jnp.inf, 0.0)
            real = (jnp.arange(n_dec * SAMPLE_CHUNK) % SAMPLE_CHUNK) < t_dec
            g_pad = jnp.where(real[:, None], pad(g), idle[None, :])
            gt_pad = jnp.where(real[None, :], pad(gt.T).T, idle[:, None])
            m0 = jnp.pad(state_m[l], ((0, 0), (0, LANES - ml_heads))).reshape(n_dec, 1, LANES)
            a_s, c_s, n_s, m_s = _mlstm_scan(pad(q), pad(k), pad(v), pad(o), g_pad, gt_pad, mlstm_h_gain[l],
                                             state_C[l], state_n[l], m0, n_dec, SAMPLE_CHUNK, ml_heads)
            a_s = a_s.reshape(n_dec, SAMPLE_CHUNK, -1)[:, :t_dec].reshape(n_sample_rows, -1)
            outs["cp"].append(c_p); outs["np"].append(n_p); outs["mp"].append(m_p[:, 0, :ml_heads])
            outs["cs"].append(c_s); outs["ns"].append(n_s); outs["ms"].append(m_s[:, 0, :ml_heads])
            w_out = mlstm_w_out[l].astype(BF16)
        w_up, w_down = mlp_w_up[i].astype(BF16), mlp_w_down[i].astype(BF16)
        xp = _mix_out_mlp(xp, ap, w_out, norm_mlp[i], w_up, w_down, tm_p)
        xs = _mix_out_mlp(xs, a_s, w_out, norm_mlp[i], w_up, w_down, n_sample_rows)

    kv_p = (len(outs["kp"]), batch, seq, att_heads, att_dh)
    kv_s = (len(outs["ks"]), n_dec, t_dec, att_heads, att_dh)
    return (xp.reshape(batch, seq, dm), xs.reshape(n_dec, t_dec, dm),
            jnp.stack(outs["kp"]).reshape(kv_p), jnp.stack(outs["vp"]).reshape(kv_p),
            jnp.stack(outs["cp"]), jnp.stack(outs["np"]), jnp.stack(outs["mp"]),
            jnp.stack(outs["ks"]).reshape(kv_s), jnp.stack(outs["vs"]).reshape(kv_s),
            jnp.stack(outs["cs"]), jnp.stack(outs["ns"]), jnp.stack(outs["ms"]))
```

```python
import functools
import math

import jax
import jax.numpy as jnp
from jax import lax
from jax.experimental import pallas as pl
from jax.experimental.pallas import tpu as pltpu

F32 = jnp.float32
BF16 = jnp.bfloat16
HIGHEST = lax.Precision.HIGHEST

RMS_EPS = 1e-6
MOBA_BLOCK = 256
MOBA_TOPK = 3
LANES = 128
SUBLANES = 8
SAMPLE_CHUNK = 128
PROMPT_CHUNK = 256
ATTN_HEADS_PER_STEP = 2
NEG = -1e30
VMEM_LIMIT = 56 * 1024 * 1024


def _params(*sem):
    return pltpu.CompilerParams(dimension_semantics=sem, vmem_limit_bytes=VMEM_LIMIT)


def _dot(a, b, precision=None):
    return jnp.dot(a, b, preferred_element_type=F32, precision=precision)


def _dot_nt(a, b, precision=None):
    return lax.dot_general(a, b, (((1,), (1,)), ((), ())), preferred_element_type=F32, precision=precision)


def _dot_tn(a, b):
    return lax.dot_general(a, b, (((0,), (0,)), ((), ())), preferred_element_type=F32)


def _rms(x, g):
    return x * lax.rsqrt(jnp.mean(x * x, axis=-1, keepdims=True) + RMS_EPS) * g


def _log_sigmoid(x):
    return -(jnp.maximum(-x, 0.0) + jnp.log1p(jnp.exp(-jnp.abs(x))))


def _resident(shape):
    return pl.BlockSpec(shape, lambda *_: (0,) * len(shape), pipeline_mode=pl.Buffered(1))


def _top_blocks(g, axis):
    idx = lax.broadcasted_iota(jnp.int32, g.shape, axis).astype(F32)
    sel = jnp.zeros(g.shape, F32)
    for _ in range(MOBA_TOPK):
        m = jnp.max(g, axis=axis, keepdims=True)
        first = jnp.min(jnp.where(g == m, idx, 1e9), axis=axis, keepdims=True)
        pick = idx == jnp.where(m > -jnp.inf, first, -1.0)
        sel = jnp.where(pick, 1.0, sel)
        g = jnp.where(pick, -jnp.inf, g)
    return sel


def _moba_proj_kernel(x_ref, g_ref, w_ref, qg_ref, kg_ref, q_ref, k_ref, v_ref,
                      kb_ref=None, vt_ref=None, km_ref=None, *, n_heads, dh):
    d = n_heads * dh
    xn = _rms(x_ref[...], g_ref[...])
    qkv = _dot(xn.astype(BF16), w_ref[...])
    for h in range(n_heads):
        sl = slice(h * dh, (h + 1) * dh)
        q_ref[:, sl] = _rms(qkv[:, h * dh:(h + 1) * dh], qg_ref[...])
        kn = _rms(qkv[:, d + h * dh:d + (h + 1) * dh], kg_ref[...])
        k_ref[:, sl] = kn
        if kb_ref is not None:
            kb_ref[:, sl] = kn.astype(BF16)
    v = qkv[:, 2 * d:]
    v_ref[...] = v
    if vt_ref is not None:
        vt_ref[...] = v.T.astype(BF16)
    if km_ref is not None:
        for r in range(km_ref.shape[0]):
            km_ref[r] = jnp.mean(k_ref[r * MOBA_BLOCK:(r + 1) * MOBA_BLOCK, :], axis=0, keepdims=True)


def _moba_proj(x, gain, w_bf, q_gain, k_gain, n_heads, tm, for_prompt):
    t, dm = x.shape
    d = w_bf.shape[1] // 3
    dh = d // n_heads
    row = lambda i: (i, 0)
    out_specs = [pl.BlockSpec((tm, d), row)] * 3
    out_shape = [jax.ShapeDtypeStruct((t, d), F32)] * 3
    if for_prompt:
        out_specs += [pl.BlockSpec((tm, d), row), pl.BlockSpec((d, tm), lambda i: (0, i)),
                      pl.BlockSpec((tm // MOBA_BLOCK, 1, d), lambda i: (i, 0, 0))]
        out_shape += [jax.ShapeDtypeStruct((t, d), BF16), jax.ShapeDtypeStruct((d, t), BF16),
                      jax.ShapeDtypeStruct((t // MOBA_BLOCK, 1, d), F32)]
    return pl.pallas_call(
        functools.partial(_moba_proj_kernel, n_heads=n_heads, dh=dh),
        grid=(t // tm,),
        in_specs=[pl.BlockSpec((tm, dm), row), _resident((1, dm)), _resident((dm, 3 * d)),
                  _resident((1, dh)), _resident((1, dh))],
        out_specs=out_specs,
        out_shape=out_shape,
        compiler_params=_params("parallel"),
        name="moba_proj",
    )(x, gain.reshape(1, dm), w_bf, q_gain.reshape(1, dh), k_gain.reshape(1, dh))


def _moba_attn_kernel(q_ref, k_ref, vt_ref, km_ref, o_ref, bias_ref, *, dh, scale):
    blk = q_ref.shape[0]
    step = 2 * blk
    i = pl.program_id(2)
    heads = range(q_ref.shape[1] // dh)
    blk_i = lax.broadcasted_iota(jnp.int32, km_ref.shape[:1] + (blk,), 0)
    key_i = lax.broadcasted_iota(jnp.int32, (blk, blk), 0)
    qry_i = lax.broadcasted_iota(jnp.int32, (blk, blk), 1)
    start = pl.multiple_of(i * blk, blk)

    last_trip = k_ref.shape[0] // step - 1

    def scores(t, h, qb):
        st = pl.multiple_of(jnp.minimum(t, last_trip) * step, step)
        return _dot_nt(k_ref[pl.ds(st, step), h * dh:(h + 1) * dh], qb)

    qbs, state = [], []
    for h in heads:
        sl = slice(h * dh, (h + 1) * dh)
        q = q_ref[:, sl]
        gates = _dot_nt(km_ref[:, sl], q, HIGHEST)
        sel = _top_blocks(jnp.where(blk_i < i, gates, -jnp.inf), 0)
        bias_ref[h] = jnp.where(sel > 0, 0.0, NEG)
        qb = (q * scale).astype(BF16)
        s = _dot_nt(k_ref[pl.ds(start, blk), sl], qb)
        s = jnp.where(key_i <= qry_i, s, -jnp.inf)
        m = jnp.max(s, axis=0, keepdims=True)
        p = jnp.exp2(s - m)
        l = jnp.sum(p, axis=0, keepdims=True)
        acc = _dot(vt_ref[sl, pl.ds(start, blk)], p.astype(BF16))
        qbs.append(qb)
        state.append((m, l, acc, scores(0, h, qb)))

    def body(t, carry):
        st = pl.multiple_of(t * step, step)
        out = []
        for h in heads:
            sl = slice(h * dh, (h + 1) * dh)
            m, l, acc, s = carry[h]
            s_next = scores(t + 1, h, qbs[h])
            bias = jnp.concatenate(
                [jnp.broadcast_to(bias_ref[h, pl.ds(2 * t + u, 1), :], (blk, blk)) for u in range(2)], axis=0)
            s = s + bias
            mn = jnp.maximum(m, jnp.max(s, axis=0, keepdims=True))
            alpha = jnp.exp2(m - mn)
            p = jnp.exp2(s - mn)
            l = alpha * l + jnp.sum(p, axis=0, keepdims=True)
            acc = alpha * acc + _dot(vt_ref[sl, pl.ds(st, step)], p.astype(BF16))
            out.append((mn, l, acc, s_next))
        return tuple(out)

    final = lax.fori_loop(0, (i + 1) // 2, body, tuple(state))
    for h in heads:
        _, l, acc, _ = final[h]
        o_ref[:, h * dh:(h + 1) * dh] = (acc / l).T.astype(o_ref.dtype)


def _moba_attn(q, k_bf, vt_bf, k_mean, batch, seq, n_heads):
    d = q.shape[1]
    dh = d // n_heads
    nb = seq // MOBA_BLOCK
    nbp = k_mean.shape[0] // batch
    hb = ATTN_HEADS_PER_STEP
    assert nb % 2 == 0 and n_heads % hb == 0
    return pl.pallas_call(
        functools.partial(_moba_attn_kernel, dh=dh, scale=dh ** -0.5 * math.log2(math.e)),
        grid=(batch, n_heads // hb, nb),
        in_specs=[pl.BlockSpec((MOBA_BLOCK, hb * dh), lambda b, h, i: (b * nb + i, h)),
                  pl.BlockSpec((seq, hb * dh), lambda b, h, i: (b, h)),
                  pl.BlockSpec((hb * dh, seq), lambda b, h, i: (h, b)),
                  pl.BlockSpec((nbp, hb * dh), lambda b, h, i: (b, h))],
        out_specs=pl.BlockSpec((MOBA_BLOCK, hb * dh), lambda b, h, i: (b * nb + i, h)),
        out_shape=jax.ShapeDtypeStruct((batch * seq, d), BF16),
        scratch_shapes=[pltpu.VMEM((hb, nbp, MOBA_BLOCK), F32)],
        compiler_params=_params("parallel", "parallel", "arbitrary"),
        name="moba_attn",
    )(q, k_bf, vt_bf, k_mean)


def _sum_row_groups(x, rows):
    while x.shape[0] > rows:
        half = x.shape[0] // 2
        x = x[:half] + x[half:]
    return x


def _moba_sample_kernel(pt_ref, q_ref, kn_ref, vn_ref, k0_ref, k1_ref, v0_ref, v1_ref, o_ref,
                        mask_ref, g_all, m_all, l_all, o_all, *, n_heads, scale):
    del pt_ref
    j = pl.program_id(1)
    nblk = pl.num_programs(1)
    r, dh = q_ref.shape
    t = r // n_heads
    rows = k0_ref.shape[0]

    @pl.when(j == 0)
    def _():
        row_head = lax.broadcasted_iota(jnp.int32, (r, rows), 0) // t
        col_head = lax.rem(lax.broadcasted_iota(jnp.int32, (r, rows), 1), n_heads)
        mask_ref[...] = jnp.where(row_head == col_head, 0.0, -jnp.inf)

    q = q_ref[...]
    qb = q.astype(BF16)
    mask = mask_ref[...]
    k0 = k0_ref[...]
    k1 = k1_ref[...]
    s0 = _dot_nt(qb, k0.astype(BF16)) * scale + mask
    s1 = _dot_nt(qb, k1.astype(BF16)) * scale + mask
    m = jnp.maximum(jnp.max(s0, axis=1, keepdims=True), jnp.max(s1, axis=1, keepdims=True))
    p0 = jnp.exp(s0 - m)
    p1 = jnp.exp(s1 - m)
    l = jnp.sum(p0, axis=1, keepdims=True) + jnp.sum(p1, axis=1, keepdims=True)
    o = _dot(p0.astype(BF16), v0_ref[...].astype(BF16)) + _dot(p1.astype(BF16), v1_ref[...].astype(BF16))
    k_mean = (_sum_row_groups(k0, n_heads) + _sum_row_groups(k1, n_heads)) * (n_heads / (2.0 * rows))
    k_mean = jnp.concatenate([jnp.broadcast_to(k_mean[h:h + 1, :], (t, dh)) for h in range(n_heads)], axis=0)
    gate = jnp.sum(q * k_mean, axis=1, keepdims=True)
    g_all[j] = jnp.broadcast_to(gate, (r, LANES))
    m_all[j] = jnp.broadcast_to(m, (r, LANES))
    l_all[j] = jnp.broadcast_to(l, (r, LANES))
    o_all[j] = o

    @pl.when(j == nblk - 1)
    def _():
        pad = jnp.zeros((LANES - r, dh), F32)
        kp = jnp.concatenate([kn_ref[...], pad], axis=0).astype(BF16)
        vp = jnp.concatenate([vn_ref[...], pad], axis=0).astype(BF16)
        so = _dot_nt(qb, kp) * scale
        row = lax.broadcasted_iota(jnp.int32, so.shape, 0)
        col = lax.broadcasted_iota(jnp.int32, so.shape, 1)
        same_head = (row // t) == lax.rem(col, n_heads)
        causal = (col // n_heads) <= lax.rem(row, t)
        so = jnp.where(same_head, jnp.where(causal, so, -jnp.inf), -jnp.inf)
        m_own = jnp.max(so, axis=1, keepdims=True)
        po = jnp.exp(so - m_own)
        l_own = jnp.sum(po, axis=1, keepdims=True)
        o_own = _dot(po.astype(BF16), vp)

        sel = _top_blocks(g_all[...], 0)
        ma = m_all[...]
        mm = jnp.maximum(jnp.max(jnp.where(sel > 0, ma, -jnp.inf), axis=0), m_own)
        w = jnp.where(sel > 0, jnp.exp(ma - mm), 0.0)
        w_own = jnp.exp(m_own - mm)
        den = jnp.sum(w * l_all[...], axis=0) + w_own * l_own
        num = jnp.sum(w * o_all[...], axis=0) + w_own * o_own
        o_ref[...] = num / den


def _moba_sample(q, k_new, v_new, cache_k, cache_v, page_table, layer, n_seq, n_heads):
    n_layers, pool, page, _, dh = cache_k.shape
    t = q.shape[0] // n_seq
    r = n_heads * t
    rows = page * n_heads
    assert MOBA_BLOCK == 2 * page and r <= LANES and dh == LANES
    nblk = page_table.shape[1] // 2
    assert nblk >= MOBA_TOPK
    ck = cache_k.reshape(n_layers, pool, rows, dh)
    cv = cache_v.reshape(n_layers, pool, rows, dh)
    q_ht = q.reshape(n_seq, t, n_heads, dh).transpose(0, 2, 1, 3).reshape(n_seq * r, dh)
    tok = pl.BlockSpec((r, dh), lambda s, j, pt: (s, 0))
    pg = lambda o: pl.BlockSpec((None, None, rows, dh), lambda s, j, pt: (layer, pt[s, 2 * j + o], 0, 0))
    stat = pltpu.VMEM((nblk, r, LANES), F32)
    out = pl.pallas_call(
        functools.partial(_moba_sample_kernel, n_heads=n_heads, scale=dh ** -0.5),
        grid_spec=pltpu.PrefetchScalarGridSpec(
            num_scalar_prefetch=1,
            grid=(n_seq, nblk),
            in_specs=[tok, tok, tok, pg(0), pg(1), pg(0), pg(1)],
            out_specs=tok,
            scratch_shapes=[pltpu.VMEM((r, rows), F32), stat, stat, stat, stat]),
        out_shape=jax.ShapeDtypeStruct((n_seq * r, dh), F32),
        compiler_params=_params("parallel", "arbitrary"),
        name="moba_sample",
    )(page_table, q_ht, k_new.reshape(n_seq * r, dh), v_new.reshape(n_seq * r, dh), ck, ck, cv, cv)
    return out.reshape(n_seq, n_heads, t, dh).transpose(0, 2, 1, 3).reshape(n_seq * t, n_heads * dh)


def _mlstm_proj_kernel(x_ref, g_ref, w_ref, wg_ref, bg_ref,
                       q_ref, k_ref, v_ref, o_ref, gt_ref, gtt_ref, *, n_heads, hk, hv):
    xn = _rms(x_ref[...], g_ref[...])
    z = _dot(xn.astype(BF16), w_ref[...])
    dk = hk // n_heads
    q_ref[...] = z[:, :hk].astype(BF16)
    k_ref[...] = (z[:, hk:2 * hk] * dk ** -0.5).astype(BF16)
    v_ref[...] = z[:, 2 * hk:2 * hk + hv].astype(BF16)
    o_ref[...] = jax.nn.sigmoid(z[:, 2 * hk + hv:])
    g = _dot(xn, wg_ref[...], HIGHEST) + bg_ref[...]
    c = lax.broadcasted_iota(jnp.int32, g.shape, 1)
    g = jnp.where((c >= n_heads) & (c < 2 * n_heads), _log_sigmoid(g), g)
    gt_ref[...] = g
    gtt_ref[...] = g.T


def _mlstm_proj(x, gain, w_in, b_gates, n_heads, tm):
    t, dm = x.shape
    ng = 2 * n_heads
    hv = dm
    hk = (w_in.shape[1] - ng - 2 * hv) // 2
    w_main = w_in[:, :2 * hk + 2 * hv].astype(BF16)
    w_g = jnp.pad(w_in[:, 2 * hk + 2 * hv:], ((0, 0), (0, LANES - ng)))
    b_g = jnp.pad(b_gates, (0, LANES - ng))
    row = lambda i: (i, 0)
    return pl.pallas_call(
        functools.partial(_mlstm_proj_kernel, n_heads=n_heads, hk=hk, hv=hv),
        grid=(t // tm,),
        in_specs=[pl.BlockSpec((tm, dm), row), _resident((1, dm)), _resident((dm, 2 * hk + 2 * hv)),
                  _resident((dm, LANES)), _resident((1, LANES))],
        out_specs=[pl.BlockSpec((tm, hk), row), pl.BlockSpec((tm, hk), row), pl.BlockSpec((tm, hv), row),
                   pl.BlockSpec((tm, hv), row), pl.BlockSpec((tm, LANES), row),
                   pl.BlockSpec((LANES, tm), lambda i: (0, i))],
        out_shape=[jax.ShapeDtypeStruct((t, hk), BF16), jax.ShapeDtypeStruct((t, hk), BF16),
                   jax.ShapeDtypeStruct((t, hv), BF16), jax.ShapeDtypeStruct((t, hv), F32),
                   jax.ShapeDtypeStruct((t, LANES), F32), jax.ShapeDtypeStruct((LANES, t), F32)],
        compiler_params=_params("parallel"),
        name="mlstm_proj",
    )(x, gain.reshape(1, dm), w_main, w_g, b_g.reshape(1, LANES))


def _mlstm_kernel(q_ref, k_ref, v_ref, o_ref, g_ref, gt_ref, hg_ref, c0_ref, n0_ref, m0_ref,
                  y_ref, c_ref, n_ref, m_ref, *, n_heads):
    @pl.when(pl.program_id(1) == 0)
    def _():
        c_ref[...] = c0_ref[...]
        n_ref[...] = n0_ref[...]
        m_ref[...] = m0_ref[...]

    ln = q_ref.shape[0]
    dk = q_ref.shape[1] // n_heads
    dv = v_ref.shape[1] // n_heads
    ti = lax.broadcasted_iota(jnp.int32, (ln, ln), 0)
    si = lax.broadcasted_iota(jnp.int32, (ln, ln), 1)
    tri = si <= ti
    g = g_ref[...]
    gt = gt_ref[...]
    m_row = m_ref[0]
    lane = lax.broadcasted_iota(jnp.int32, m_row.shape, 1)
    for h in range(n_heads):
        ig_row, fg_row = gt[h:h + 1, :], gt[n_heads + h:n_heads + h + 1, :]
        ig_col, fg_col = g[:, h:h + 1], g[:, n_heads + h:n_heads + h + 1]
        bcum_col = jnp.sum(jnp.where(tri, fg_row, 0.0), axis=1, keepdims=True)
        bcum_row = jnp.sum(jnp.where(ti <= si, fg_col, 0.0), axis=0, keepdims=True)
        a_row = ig_row - bcum_row
        a_col = ig_col - bcum_col
        m_prev = m_row[:, h:h + 1]
        mx = jnp.maximum(m_prev, jnp.max(jnp.where(tri, a_row, -jnp.inf), axis=1, keepdims=True))
        dmat = jnp.exp(jnp.where(tri, a_row - mx, -jnp.inf))
        inter = jnp.exp(m_prev - mx)
        qh = q_ref[:, h * dk:(h + 1) * dk]
        kh = k_ref[:, h * dk:(h + 1) * dk]
        vh = v_ref[:, h * dv:(h + 1) * dv]
        w = dmat * _dot_nt(qh, kh)
        c_prev = c_ref[0, h]
        n_prev = n_ref[0, h:h + 1, :]
        num = inter * _dot_nt(qh, c_prev.astype(BF16)) + _dot(w.astype(BF16), vh)
        den = (inter * jnp.sum(qh.astype(F32) * n_prev, axis=1, keepdims=True)
               + jnp.sum(w, axis=1, keepdims=True))
        hh = num / jnp.maximum(jnp.abs(den), jnp.exp(-(bcum_col + mx)))
        mx_last = mx[ln - 1:ln, :]
        decay = jnp.exp(a_col - mx_last)
        carry = jnp.exp(m_prev - mx_last)
        c_ref[0, h] = carry * c_prev + _dot_tn((decay * vh.astype(F32)).astype(BF16), kh)
        n_ref[0, h:h + 1, :] = carry * n_prev + jnp.sum(decay * kh.astype(F32), axis=0, keepdims=True)
        m_row = jnp.where(lane == h, bcum_col[ln - 1:ln, :] + mx_last, m_row)
        hn = _rms(hh, hg_ref[:, h * dv:(h + 1) * dv])
        y_ref[:, h * dv:(h + 1) * dv] = (hn * o_ref[:, h * dv:(h + 1) * dv]).astype(y_ref.dtype)
    m_ref[0] = m_row


def _mlstm_scan(q, k, v, o, gates, gates_t, h_gain, c0, n0, m0, n_seq, chunk, n_heads):
    t, hk = q.shape
    hv = v.shape[1]
    nc = t // n_seq // chunk
    row = lambda s, c: (s * nc + c, 0)
    st4 = pl.BlockSpec((1,) + c0.shape[1:], lambda s, c: (s, 0, 0, 0))
    st3n = pl.BlockSpec((1,) + n0.shape[1:], lambda s, c: (s, 0, 0))
    st3m = pl.BlockSpec((1, 1, LANES), lambda s, c: (s, 0, 0))
    return pl.pallas_call(
        functools.partial(_mlstm_kernel, n_heads=n_heads),
        grid=(n_seq, nc),
        in_specs=[pl.BlockSpec((chunk, hk), row), pl.BlockSpec((chunk, hk), row), pl.BlockSpec((chunk, hv), row),
                  pl.BlockSpec((chunk, hv), row), pl.BlockSpec((chunk, LANES), row),
                  pl.BlockSpec((LANES, chunk), lambda s, c: (0, s * nc + c)),
                  pl.BlockSpec((1, hv), lambda s, c: (0, 0)), st4, st3n, st3m],
        out_specs=[pl.BlockSpec((chunk, hv), row), st4, st3n, st3m],
        out_shape=[jax.ShapeDtypeStruct((t, hv), BF16), jax.ShapeDtypeStruct(c0.shape, F32),
                   jax.ShapeDtypeStruct(n0.shape, F32), jax.ShapeDtypeStruct(m0.shape, F32)],
        compiler_params=_params("parallel", "arbitrary"),
        name="mlstm_scan",
    )(q, k, v, o, gates, gates_t, h_gain.reshape(1, hv), c0, n0, m0)


def _pad_rows(a, n_seq, chunk):
    t = a.shape[0] // n_seq
    a = a.reshape(n_seq, t, a.shape[1])
    a = jnp.pad(a, ((0, 0), (0, chunk - t), (0, 0)))
    return a.reshape(n_seq * chunk, a.shape[2])


def _mlp_kernel(x_ref, a_ref, wo_ref, g_ref, wu_ref, wd_ref, out_ref, *, ff_chunk):
    x1 = x_ref[...] + _dot(a_ref[...].astype(BF16), wo_ref[...])
    xn = _rms(x1, g_ref[...]).astype(BF16)
    acc = x1
    for c in range(wu_ref.shape[1] // ff_chunk):
        hid = _dot(xn, wu_ref[:, c * ff_chunk:(c + 1) * ff_chunk])
        hid = jnp.square(jnp.maximum(hid, 0.0)).astype(BF16)
        acc = acc + _dot(hid, wd_ref[c * ff_chunk:(c + 1) * ff_chunk, :])
    out_ref[...] = acc


def _mix_out_mlp(x, a, w_out_bf, gain, w_up_bf, w_down_bf, tm):
    t, dm = x.shape
    dff = w_up_bf.shape[1]
    row = lambda i: (i, 0)
    return pl.pallas_call(
        functools.partial(_mlp_kernel, ff_chunk=1024),
        grid=(t // tm,),
        in_specs=[pl.BlockSpec((tm, dm), row), pl.BlockSpec((tm, a.shape[1]), row), _resident(w_out_bf.shape),
                  _resident((1, dm)), _resident((dm, dff)), _resident((dff, dm))],
        out_specs=pl.BlockSpec((tm, dm), row),
        out_shape=jax.ShapeDtypeStruct((t, dm), F32),
        compiler_params=_params("parallel"),
        name="mix_out_mlp",
    )(x, a, w_out_bf, gain.reshape(1, dm), w_up_bf, w_down_bf)


def kernel(x_prompt, x_sample, cache_k, cache_v, state_C, state_n, state_m, page_table, norm_mix, norm_mlp,
           moba_w_qkv, moba_q_gain, moba_k_gain, moba_w_out, mlstm_w_in, mlstm_b_gates, mlstm_h_gain,
           mlstm_w_out, mlp_w_up, mlp_w_down):
    batch, seq, dm = x_prompt.shape
    n_dec, t_dec, _ = x_sample.shape
    depth = norm_mix.shape[0]
    att_heads, att_dh = cache_k.shape[3], cache_k.shape[4]
    ml_heads, ml_dv, ml_dk = state_C.shape[2], state_C.shape[3], state_C.shape[4]
    n_prompt_rows, n_sample_rows = batch * seq, n_dec * t_dec
    tm_p = 512
    nb = seq // MOBA_BLOCK
    nbp = -(-nb // SUBLANES) * SUBLANES
    assert seq % max(MOBA_BLOCK, PROMPT_CHUNK, tm_p) == 0 and t_dec <= SAMPLE_CHUNK

    xp = x_prompt.reshape(n_prompt_rows, dm)
    xs = x_sample.reshape(n_sample_rows, dm)
    outs = {name: [] for name in ("kp", "vp", "cp", "np", "mp", "ks", "vs", "cs", "ns", "ms")}

    for i in range(depth):
        l = i // 2
        if i % 2 == 0:
            w_qkv = moba_w_qkv[l].astype(BF16)
            qp, kp, vp, kp_bf, vtp_bf, kmean = _moba_proj(xp, norm_mix[i], w_qkv, moba_q_gain[l], moba_k_gain[l],
                                                          att_heads, tm_p, True)
            kmean = jnp.pad(kmean.reshape(batch, nb, -1), ((0, 0), (0, nbp - nb), (0, 0)))
            ap = _moba_attn(qp, kp_bf, vtp_bf, kmean.reshape(batch * nbp, -1), batch, seq, att_heads)
            qs, ks, vs = _moba_proj(xs, norm_mix[i], w_qkv, moba_q_gain[l], moba_k_gain[l],
                                    att_heads, n_sample_rows, False)
            a_s = _moba_sample(qs, ks, vs, cache_k, cache_v, page_table, l, n_dec, att_heads)
            outs["kp"].append(kp); outs["vp"].append(vp); outs["ks"].append(ks); outs["vs"].append(vs)
            w_out = moba_w_out[l].astype(BF16)
        else:
            zc = jnp.zeros((batch, ml_heads, ml_dv, ml_dk), F32)
            zn = jnp.zeros((batch, ml_heads, ml_dk), F32)
            zm = jnp.zeros((batch, 1, LANES), F32)
            q, k, v, o, g, gt = _mlstm_proj(xp, norm_mix[i], mlstm_w_in[l], mlstm_b_gates[l], ml_heads, tm_p)
            ap, c_p, n_p, m_p = _mlstm_scan(q, k, v, o, g, gt, mlstm_h_gain[l], zc, zn, zm,
                                            batch, PROMPT_CHUNK, ml_heads)
            q, k, v, o, g, gt = _mlstm_proj(xs, norm_mix[i], mlstm_w_in[l], mlstm_b_gates[l], ml_heads,
                                            n_sample_rows)
            pad = lambda a: _pad_rows(a, n_dec, SAMPLE_CHUNK)
            idle = jnp.where(jnp.arange(LANES) < ml_heads, -jnp.inf, 0.0)
            real = (jnp.arange(n_dec * SAMPLE_CHUNK) % SAMPLE_CHUNK) < t_dec
            g_pad = jnp.where(real[:, None], pad(g), idle[None, :])
            gt_pad = jnp.where(real[None, :], pad(gt.T).T, idle[:, None])
            m0 = jnp.pad(state_m[l], ((0, 0), (0, LANES - ml_heads))).reshape(n_dec, 1, LANES)
            a_s, c_s, n_s, m_s = _mlstm_scan(pad(q), pad(k), pad(v), pad(o), g_pad, gt_pad, mlstm_h_gain[l],
                                             state_C[l], state_n[l], m0, n_dec, SAMPLE_CHUNK, ml_heads)
            a_s = a_s.reshape(n_dec, SAMPLE_CHUNK, -1)[:, :t_dec].reshape(n_sample_rows, -1)
            outs["cp"].append(c_p); outs["np"].append(n_p); outs["mp"].append(m_p[:, 0, :ml_heads])
            outs["cs"].append(c_s); outs["ns"].append(n_s); outs["ms"].append(m_s[:, 0, :ml_heads])
            w_out = mlstm_w_out[l].astype(BF16)
        w_up, w_down = mlp_w_up[i].astype(BF16), mlp_w_down[i].astype(BF16)
        xp = _mix_out_mlp(xp, ap, w_out, norm_mlp[i], w_up, w_down, tm_p)
        xs = _mix_out_mlp(xs, a_s, w_out, norm_mlp[i], w_up, w_down, n_sample_rows)

    kv_p = (len(outs["kp"]), batch, seq, att_heads, att_dh)
    kv_s = (len(outs["ks"]), n_dec, t_dec, att_heads, att_dh)
    return (xp.reshape(batch, seq, dm), xs.reshape(n_dec, t_dec, dm),
            jnp.stack(outs["kp"]).reshape(kv_p), jnp.stack(outs["vp"]).reshape(kv_p),
            jnp.stack(outs["cp"]), jnp.stack(outs["np"]), jnp.stack(outs["mp"]),
            jnp.stack(outs["ks"]).reshape(kv_s), jnp.stack(outs["vs"]).reshape(kv_s),
            jnp.stack(outs["cs"]), jnp.stack(outs["ns"]), jnp.stack(outs["ms"]))
```

```python
import functools
import math

import jax
import jax.numpy as jnp
from jax import lax
from jax.experimental import pallas as pl
from jax.experimental.pallas import tpu as pltpu

F32 = jnp.float32
BF16 = jnp.bfloat16
HIGHEST = lax.Precision.HIGHEST

RMS_EPS = 1e-6
MOBA_BLOCK = 256
MOBA_TOPK = 3
LANES = 128
SUBLANES = 8
BF16_ROWS = 16
SAMPLE_CHUNK = 128
PROMPT_CHUNK = 256
ATTN_HEADS_PER_STEP = 2
SAMPLE_BLOCKS_PER_STEP = 2
NEG = -1e30
VMEM_LIMIT = 56 * 1024 * 1024


def _params(*sem):
    return pltpu.CompilerParams(dimension_semantics=sem, vmem_limit_bytes=VMEM_LIMIT)


def _dot(a, b, precision=None):
    return jnp.dot(a, b, preferred_element_type=F32, precision=precision)


def _dot_nt(a, b, precision=None):
    return lax.dot_general(a, b, (((1,), (1,)), ((), ())), preferred_element_type=F32, precision=precision)


def _dot_tn(a, b):
    return lax.dot_general(a, b, (((0,), (0,)), ((), ())), preferred_element_type=F32)


def _rms(x, g):
    return x * lax.rsqrt(jnp.mean(x * x, axis=-1, keepdims=True) + RMS_EPS) * g


def _log_sigmoid(x):
    return -(jnp.maximum(-x, 0.0) + jnp.log1p(jnp.exp(-jnp.abs(x))))


def _resident(shape):
    return pl.BlockSpec(shape, lambda *_: (0,) * len(shape), pipeline_mode=pl.Buffered(1))


def _top_blocks(g, axis):
    idx = lax.broadcasted_iota(jnp.int32, g.shape, axis).astype(F32)
    sel = jnp.zeros(g.shape, F32)
    for _ in range(MOBA_TOPK):
        m = jnp.max(g, axis=axis, keepdims=True)
        first = jnp.min(jnp.where(g == m, idx, 1e9), axis=axis, keepdims=True)
        pick = idx == jnp.where(m > -jnp.inf, first, -1.0)
        sel = jnp.where(pick, 1.0, sel)
        g = jnp.where(pick, -jnp.inf, g)
    return sel


def _moba_proj_kernel(x_ref, g_ref, w_ref, qg_ref, kg_ref, q_ref, k_ref, v_ref,
                      kb_ref=None, vt_ref=None, km_ref=None, *, n_heads, dh):
    d = n_heads * dh
    xn = _rms(x_ref[...], g_ref[...])
    qkv = _dot(xn.astype(BF16), w_ref[...])
    for h in range(n_heads):
        sl = slice(h * dh, (h + 1) * dh)
        q_ref[:, sl] = _rms(qkv[:, h * dh:(h + 1) * dh], qg_ref[...])
        kn = _rms(qkv[:, d + h * dh:d + (h + 1) * dh], kg_ref[...])
        k_ref[:, sl] = kn
        if kb_ref is not None:
            kb_ref[:, sl] = kn.astype(BF16)
    v = qkv[:, 2 * d:]
    v_ref[...] = v
    if vt_ref is not None:
        vt_ref[...] = v.T.astype(BF16)
    if km_ref is not None:
        for r in range(km_ref.shape[0]):
            km_ref[r] = jnp.mean(k_ref[r * MOBA_BLOCK:(r + 1) * MOBA_BLOCK, :], axis=0, keepdims=True)


def _moba_proj(x, gain, w_bf, q_gain, k_gain, n_heads, tm, for_prompt):
    t, dm = x.shape
    d = w_bf.shape[1] // 3
    dh = d // n_heads
    row = lambda i: (i, 0)
    out_specs = [pl.BlockSpec((tm, d), row)] * 3
    out_shape = [jax.ShapeDtypeStruct((t, d), F32)] * 3
    if for_prompt:
        out_specs += [pl.BlockSpec((tm, d), row), pl.BlockSpec((d, tm), lambda i: (0, i)),
                      pl.BlockSpec((tm // MOBA_BLOCK, 1, d), lambda i: (i, 0, 0))]
        out_shape += [jax.ShapeDtypeStruct((t, d), BF16), jax.ShapeDtypeStruct((d, t), BF16),
                      jax.ShapeDtypeStruct((t // MOBA_BLOCK, 1, d), F32)]
    return pl.pallas_call(
        functools.partial(_moba_proj_kernel, n_heads=n_heads, dh=dh),
        grid=(t // tm,),
        in_specs=[pl.BlockSpec((tm, dm), row), _resident((1, dm)), _resident((dm, 3 * d)),
                  _resident((1, dh)), _resident((1, dh))],
        out_specs=out_specs,
        out_shape=out_shape,
        compiler_params=_params("parallel"),
        name="moba_proj",
    )(x, gain.reshape(1, dm), w_bf, q_gain.reshape(1, dh), k_gain.reshape(1, dh))


def _moba_attn_kernel(q_ref, k_ref, vt_ref, km_ref, o_ref, bias_ref, *, dh, scale):
    blk = q_ref.shape[0]
    step = 2 * blk
    i = pl.program_id(2)
    heads = range(q_ref.shape[1] // dh)
    blk_i = lax.broadcasted_iota(jnp.int32, km_ref.shape[:1] + (blk,), 0)
    key_i = lax.broadcasted_iota(jnp.int32, (blk, blk), 0)
    qry_i = lax.broadcasted_iota(jnp.int32, (blk, blk), 1)
    start = pl.multiple_of(i * blk, blk)
    last_trip = k_ref.shape[0] // step - 1

    def scores(t, h, qb):
        st = pl.multiple_of(jnp.minimum(t, last_trip) * step, step)
        return tuple(_dot_nt(k_ref[pl.ds(pl.multiple_of(st + u * blk, blk), blk), h * dh:(h + 1) * dh], qb)
                     for u in range(2))

    qbs, state = [], []
    for h in heads:
        sl = slice(h * dh, (h + 1) * dh)
        q = q_ref[:, sl]
        gates = _dot_nt(km_ref[:, sl], q, HIGHEST)
        sel = _top_blocks(jnp.where(blk_i < i, gates, -jnp.inf), 0)
        bias_ref[h] = jnp.where(sel > 0, 0.0, NEG)
        qb = (q * scale).astype(BF16)
        s = _dot_nt(k_ref[pl.ds(start, blk), sl], qb)
        s = jnp.where(key_i <= qry_i, s, -jnp.inf)
        m = jnp.max(s, axis=0, keepdims=True)
        p = jnp.exp2(s - m)
        l = jnp.sum(p, axis=0, keepdims=True)
        acc = _dot(vt_ref[sl, pl.ds(start, blk)], p.astype(BF16))
        qbs.append(qb)
        state.append((m, l, acc, scores(0, h, qb)))

    def body(t, carry):
        st = pl.multiple_of(t * step, step)
        out = []
        for h in heads:
            sl = slice(h * dh, (h + 1) * dh)
            m, l, acc, s = carry[h]
            s_next = scores(t + 1, h, qbs[h])
            s = [s[u] + bias_ref[h, pl.ds(2 * t + u, 1), :] for u in range(2)]
            mn = jnp.maximum(m, jnp.maximum(jnp.max(s[0], axis=0, keepdims=True),
                                            jnp.max(s[1], axis=0, keepdims=True)))
            alpha = jnp.exp2(m - mn)
            p = [jnp.exp2(s[u] - mn) for u in range(2)]
            l = alpha * l + jnp.sum(p[0], axis=0, keepdims=True) + jnp.sum(p[1], axis=0, keepdims=True)
            acc = alpha * acc + sum(_dot(vt_ref[sl, pl.ds(pl.multiple_of(st + u * blk, blk), blk)], p[u].astype(BF16))
                                    for u in range(2))
            out.append((mn, l, acc, s_next))
        return tuple(out)

    final = lax.fori_loop(0, (i + 1) // 2, body, tuple(state))
    for h in heads:
        _, l, acc, _ = final[h]
        o_ref[:, h * dh:(h + 1) * dh] = (acc / l).T.astype(o_ref.dtype)


def _moba_attn(q, k_bf, vt_bf, k_mean, batch, seq, n_heads):
    d = q.shape[1]
    dh = d // n_heads
    nb = seq // MOBA_BLOCK
    nbp = k_mean.shape[0] // batch
    hb = ATTN_HEADS_PER_STEP
    assert nb % 2 == 0 and n_heads % hb == 0
    return pl.pallas_call(
        functools.partial(_moba_attn_kernel, dh=dh, scale=dh ** -0.5 * math.log2(math.e)),
        grid=(batch, n_heads // hb, nb),
        in_specs=[pl.BlockSpec((MOBA_BLOCK, hb * dh), lambda b, h, i: (b * nb + i, h)),
                  pl.BlockSpec((seq, hb * dh), lambda b, h, i: (b, h)),
                  pl.BlockSpec((hb * dh, seq), lambda b, h, i: (h, b)),
                  pl.BlockSpec((nbp, hb * dh), lambda b, h, i: (b, h))],
        out_specs=pl.BlockSpec((MOBA_BLOCK, hb * dh), lambda b, h, i: (b * nb + i, h)),
        out_shape=jax.ShapeDtypeStruct((batch * seq, d), BF16),
        scratch_shapes=[pltpu.VMEM((hb, nbp, MOBA_BLOCK), F32)],
        compiler_params=_params("parallel", "parallel", "arbitrary"),
        name="moba_attn",
    )(q, k_bf, vt_bf, k_mean)


def _sum_row_groups(x, rows):
    while x.shape[0] > rows:
        half = x.shape[0] // 2
        x = x[:half] + x[half:]
    return x


def _moba_sample_kernel(pt_ref, q_ref, kn_ref, vn_ref, *refs, n_heads, scale):
    del pt_ref
    bps = SAMPLE_BLOCKS_PER_STEP
    k_refs, v_refs = refs[:2 * bps], refs[2 * bps:4 * bps]
    o_ref, mask_ref, g_all, m_all, l_all, o_all = refs[4 * bps:]
    j = pl.program_id(1)
    r, dh = q_ref.shape
    t = r // n_heads
    rows = k_refs[0].shape[0]

    @pl.when(j == 0)
    def _():
        row_head = lax.broadcasted_iota(jnp.int32, (r, rows), 0) // t
        col_head = lax.rem(lax.broadcasted_iota(jnp.int32, (r, rows), 1), n_heads)
        mask_ref[...] = jnp.where(row_head == col_head, 0.0, -jnp.inf)

    q = q_ref[...]
    qb = q.astype(BF16)
    mask = mask_ref[...]
    for u in range(bps):
        k0 = k_refs[2 * u][...]
        k1 = k_refs[2 * u + 1][...]
        s0 = _dot_nt(qb, k0.astype(BF16)) * scale + mask
        s1 = _dot_nt(qb, k1.astype(BF16)) * scale + mask
        m = jnp.maximum(jnp.max(s0, axis=1, keepdims=True), jnp.max(s1, axis=1, keepdims=True))
        p0 = jnp.exp(s0 - m)
        p1 = jnp.exp(s1 - m)
        l = jnp.sum(p0, axis=1, keepdims=True) + jnp.sum(p1, axis=1, keepdims=True)
        o = (_dot(p0.astype(BF16), v_refs[2 * u][...].astype(BF16))
             + _dot(p1.astype(BF16), v_refs[2 * u + 1][...].astype(BF16)))
        k_mean = (_sum_row_groups(k0, n_heads) + _sum_row_groups(k1, n_heads)) * (n_heads / (2.0 * rows))
        k_mean = jnp.concatenate([jnp.broadcast_to(k_mean[h:h + 1, :], (t, dh)) for h in range(n_heads)], axis=0)
        gate = jnp.sum(q * k_mean, axis=1, keepdims=True)
        g_all[bps * j + u] = jnp.broadcast_to(gate, (r, LANES))
        m_all[bps * j + u] = jnp.broadcast_to(m, (r, LANES))
        l_all[bps * j + u] = jnp.broadcast_to(l, (r, LANES))
        o_all[bps * j + u] = o

    @pl.when(j == pl.num_programs(1) - 1)
    def _():
        pad = jnp.zeros((LANES - r, dh), F32)
        kp = jnp.concatenate([kn_ref[...], pad], axis=0).astype(BF16)
        vp = jnp.concatenate([vn_ref[...], pad], axis=0).astype(BF16)
        so = _dot_nt(qb, kp) * scale
        row = lax.broadcasted_iota(jnp.int32, so.shape, 0)
        col = lax.broadcasted_iota(jnp.int32, so.shape, 1)
        same_head = (row // t) == lax.rem(col, n_heads)
        causal = (col // n_heads) <= lax.rem(row, t)
        so = jnp.where(same_head, jnp.where(causal, so, -jnp.inf), -jnp.inf)
        m_own = jnp.max(so, axis=1, keepdims=True)
        po = jnp.exp(so - m_own)
        l_own = jnp.sum(po, axis=1, keepdims=True)
        o_own = _dot(po.astype(BF16), vp)

        sel = _top_blocks(g_all[...], 0)
        ma = m_all[...]
        mm = jnp.maximum(jnp.max(jnp.where(sel > 0, ma, -jnp.inf), axis=0), m_own)
        w = jnp.where(sel > 0, jnp.exp(ma - mm), 0.0)
        w_own = jnp.exp(m_own - mm)
        den = jnp.sum(w * l_all[...], axis=0) + w_own * l_own
        num = jnp.sum(w * o_all[...], axis=0) + w_own * o_own
        o_ref[...] = num / den


def _moba_sample(q, k_new, v_new, cache_k, cache_v, page_table, layer, n_seq, n_heads):
    n_layers, pool, page, _, dh = cache_k.shape
    t = q.shape[0] // n_seq
    r = n_heads * t
    rows = page * n_heads
    bps = SAMPLE_BLOCKS_PER_STEP
    assert MOBA_BLOCK == 2 * page and r <= LANES and dh == LANES
    nblk = page_table.shape[1] // 2
    assert nblk >= MOBA_TOPK and nblk % bps == 0
    ck = cache_k.reshape(n_layers, pool, rows, dh)
    cv = cache_v.reshape(n_layers, pool, rows, dh)
    q_ht = q.reshape(n_seq, t, n_heads, dh).transpose(0, 2, 1, 3).reshape(n_seq * r, dh)
    tok = pl.BlockSpec((r, dh), lambda s, j, pt: (s, 0))
    pg = lambda o: pl.BlockSpec((None, None, rows, dh), lambda s, j, pt: (layer, pt[s, 2 * bps * j + o], 0, 0))
    pages = [pg(o) for o in range(2 * bps)]
    stat = pltpu.VMEM((nblk, r, LANES), F32)
    out = pl.pallas_call(
        functools.partial(_moba_sample_kernel, n_heads=n_heads, scale=dh ** -0.5),
        grid_spec=pltpu.PrefetchScalarGridSpec(
            num_scalar_prefetch=1,
            grid=(n_seq, nblk // bps),
            in_specs=[tok, tok, tok] + pages + pages,
            out_specs=tok,
            scratch_shapes=[pltpu.VMEM((r, rows), F32), stat, stat, stat, stat]),
        out_shape=jax.ShapeDtypeStruct((n_seq * r, dh), F32),
        compiler_params=_params("parallel", "arbitrary"),
        name="moba_sample",
    )(page_table, q_ht, k_new.reshape(n_seq * r, dh), v_new.reshape(n_seq * r, dh),
      *([ck] * (2 * bps)), *([cv] * (2 * bps)))
    return out.reshape(n_seq, n_heads, t, dh).transpose(0, 2, 1, 3).reshape(n_seq * t, n_heads * dh)


def _mlstm_proj_kernel(x_ref, g_ref, w_ref, wg_ref, bg_ref, q_ref, k_ref, vt_ref, o_ref, gt_ref, *, n_heads, hk, hv):
    xn = _rms(x_ref[...], g_ref[...])
    z = _dot(xn.astype(BF16), w_ref[...])
    dk = hk // n_heads
    q_ref[...] = z[:, :hk].astype(BF16)
    k_ref[...] = (z[:, hk:2 * hk] * dk ** -0.5).astype(BF16)
    vt_ref[...] = z[:, 2 * hk:2 * hk + hv].T.astype(BF16)
    o_ref[...] = jax.nn.sigmoid(z[:, 2 * hk + hv:])
    g = _dot(xn, wg_ref[...], HIGHEST) + bg_ref[...]
    c = lax.broadcasted_iota(jnp.int32, g.shape, 1)
    g = jnp.where((c >= n_heads) & (c < 2 * n_heads), _log_sigmoid(g), g)
    gt_ref[...] = g.T


def _mlstm_proj(x, gain, w_in, b_gates, n_heads, tm):
    t, dm = x.shape
    ng = 2 * n_heads
    hv = dm
    hk = (w_in.shape[1] - ng - 2 * hv) // 2
    w_main = w_in[:, :2 * hk + 2 * hv].astype(BF16)
    w_g = jnp.pad(w_in[:, 2 * hk + 2 * hv:], ((0, 0), (0, LANES - ng)))
    b_g = jnp.pad(b_gates, (0, LANES - ng))
    row = lambda i: (i, 0)
    return pl.pallas_call(
        functools.partial(_mlstm_proj_kernel, n_heads=n_heads, hk=hk, hv=hv),
        grid=(t // tm,),
        in_specs=[pl.BlockSpec((tm, dm), row), _resident((1, dm)), _resident((dm, 2 * hk + 2 * hv)),
                  _resident((dm, LANES)), _resident((1, LANES))],
        out_specs=[pl.BlockSpec((tm, hk), row), pl.BlockSpec((tm, hk), row), pl.BlockSpec((hv, tm), lambda i: (0, i)),
                   pl.BlockSpec((tm, hv), row), pl.BlockSpec((LANES, tm), lambda i: (0, i))],
        out_shape=[jax.ShapeDtypeStruct((t, hk), BF16), jax.ShapeDtypeStruct((t, hk), BF16),
                   jax.ShapeDtypeStruct((hv, t), BF16), jax.ShapeDtypeStruct((t, hv), F32),
                   jax.ShapeDtypeStruct((LANES, t), F32)],
        compiler_params=_params("parallel"),
        name="mlstm_proj",
    )(x, gain.reshape(1, dm), w_main, w_g, b_g.reshape(1, LANES))


def _scan_lanes(x, op, fill):
    lane = lax.broadcasted_iota(jnp.int32, x.shape, 1)
    d = 1
    while d < x.shape[1]:
        x = op(x, jnp.where(lane >= d, pltpu.roll(x, d, 1), fill))
        d *= 2
    return x


def _mlstm_kernel(q_ref, k_ref, vt_ref, o_ref, gt_ref, hg_ref, c0_ref, n0_ref, m0_ref,
                  y_ref, c_ref, n_ref, m_ref, *, n_heads):
    @pl.when(pl.program_id(1) == 0)
    def _():
        c_ref[...] = c0_ref[...]
        n_ref[...] = n0_ref[...]
        m_ref[...] = m0_ref[...]

    ln = q_ref.shape[0]
    dk = q_ref.shape[1] // n_heads
    dv = vt_ref.shape[0] // n_heads
    gt = gt_ref[...]
    ig, fg = gt[:n_heads], gt[n_heads:2 * n_heads]
    bcum = _scan_lanes(fg, jnp.add, 0.0)
    a = ig - bcum
    m_prev = m_ref[0][:, :1]
    mx = jnp.maximum(m_prev, _scan_lanes(a, jnp.maximum, -jnp.inf))
    mx_last = mx[:, ln - 1:]
    inter = jnp.exp(m_prev - mx)
    floor = jnp.exp(-(bcum + mx))
    decay = jnp.exp(a - mx_last)
    carry = jnp.exp(m_prev - mx_last)
    a_cols = jnp.concatenate([a, jnp.zeros((LANES - n_heads, ln), F32)], axis=0).T
    si = lax.broadcasted_iota(jnp.int32, (ln, ln), 0)
    ti = lax.broadcasted_iota(jnp.int32, (ln, ln), 1)
    visible = si <= ti
    ones = jnp.ones((BF16_ROWS, ln), BF16)
    for h in range(n_heads):
        row = slice(h, h + 1)
        qh = q_ref[:, h * dk:(h + 1) * dk]
        kh = k_ref[:, h * dk:(h + 1) * dk]
        vth = vt_ref[h * dv:(h + 1) * dv, :]
        c_prev = c_ref[0, h]
        n_prev = n_ref[0, row, :]
        dmat = jnp.exp(jnp.where(visible, a_cols[:, row] - mx[row], -jnp.inf))
        w = dmat * _dot_nt(kh, qh)
        c_aug = jnp.concatenate([c_prev, jnp.broadcast_to(n_prev, (BF16_ROWS, dk))], axis=0).astype(BF16)
        v_aug = jnp.concatenate([vth, ones], axis=0)
        x = inter[row] * _dot_nt(c_aug, qh) + _dot(v_aug, w.astype(BF16))
        hh = x[:dv] / jnp.maximum(jnp.abs(x[dv:dv + 1]), floor[row])
        hn = hh * lax.rsqrt(jnp.mean(hh * hh, axis=0, keepdims=True) + RMS_EPS)
        y = hn.T * hg_ref[:, h * dv:(h + 1) * dv] * o_ref[:, h * dv:(h + 1) * dv]
        y_ref[:, h * dv:(h + 1) * dv] = y.astype(y_ref.dtype)
        vd = jnp.concatenate([vth.astype(F32) * decay[row], jnp.broadcast_to(decay[row], (BF16_ROWS, ln))], axis=0)
        upd = _dot(vd.astype(BF16), kh)
        c_ref[0, h] = carry[row] * c_prev + upd[:dv]
        n_ref[0, row, :] = carry[row] * n_prev + upd[dv:dv + 1]
    m_ref[0] = jnp.broadcast_to(bcum[:, ln - 1:] + mx_last, (n_heads, LANES))


def _mlstm_scan(q, k, vt, o, gates_t, h_gain, c0, n0, m0, n_seq, chunk, n_heads):
    t, hk = q.shape
    hv = vt.shape[0]
    nc = t // n_seq // chunk
    row = lambda s, c: (s * nc + c, 0)
    col = lambda s, c: (0, s * nc + c)
    st4 = pl.BlockSpec((1,) + c0.shape[1:], lambda s, c: (s, 0, 0, 0))
    st3n = pl.BlockSpec((1,) + n0.shape[1:], lambda s, c: (s, 0, 0))
    st3m = pl.BlockSpec((1,) + m0.shape[1:], lambda s, c: (s, 0, 0))
    return pl.pallas_call(
        functools.partial(_mlstm_kernel, n_heads=n_heads),
        grid=(n_seq, nc),
        in_specs=[pl.BlockSpec((chunk, hk), row), pl.BlockSpec((chunk, hk), row), pl.BlockSpec((hv, chunk), col),
                  pl.BlockSpec((chunk, hv), row), pl.BlockSpec((LANES, chunk), col),
                  pl.BlockSpec((1, hv), lambda s, c: (0, 0)), st4, st3n, st3m],
        out_specs=[pl.BlockSpec((chunk, hv), row), st4, st3n, st3m],
        out_shape=[jax.ShapeDtypeStruct((t, hv), BF16), jax.ShapeDtypeStruct(c0.shape, F32),
                   jax.ShapeDtypeStruct(n0.shape, F32), jax.ShapeDtypeStruct(m0.shape, F32)],
        compiler_params=_params("parallel", "arbitrary"),
        name="mlstm_scan",
    )(q, k, vt, o, gates_t, h_gain.reshape(1, hv), c0, n0, m0)


def _pad_rows(a, n_seq, chunk):
    t = a.shape[0] // n_seq
    a = a.reshape(n_seq, t, a.shape[1])
    a = jnp.pad(a, ((0, 0), (0, chunk - t), (0, 0)))
    return a.reshape(n_seq * chunk, a.shape[2])


def _mlp_kernel(x_ref, a_ref, wo_ref, g_ref, wu_ref, wd_ref, out_ref, *, ff_chunk):
    x1 = x_ref[...] + _dot(a_ref[...].astype(BF16), wo_ref[...])
    xn = _rms(x1, g_ref[...]).astype(BF16)
    acc = x1
    for c in range(wu_ref.shape[1] // ff_chunk):
        hid = _dot(xn, wu_ref[:, c * ff_chunk:(c + 1) * ff_chunk])
        hid = jnp.square(jnp.maximum(hid, 0.0)).astype(BF16)
        acc = acc + _dot(hid, wd_ref[c * ff_chunk:(c + 1) * ff_chunk, :])
    out_ref[...] = acc


def _mix_out_mlp(x, a, w_out_bf, gain, w_up_bf, w_down_bf, tm):
    t, dm = x.shape
    dff = w_up_bf.shape[1]
    row = lambda i: (i, 0)
    return pl.pallas_call(
        functools.partial(_mlp_kernel, ff_chunk=1024),
        grid=(t // tm,),
        in_specs=[pl.BlockSpec((tm, dm), row), pl.BlockSpec((tm, a.shape[1]), row), _resident(w_out_bf.shape),
                  _resident((1, dm)), _resident((dm, dff)), _resident((dff, dm))],
        out_specs=pl.BlockSpec((tm, dm), row),
        out_shape=jax.ShapeDtypeStruct((t, dm), F32),
        compiler_params=_params("parallel"),
        name="mix_out_mlp",
    )(x, a, w_out_bf, gain.reshape(1, dm), w_up_bf, w_down_bf)


def kernel(x_prompt, x_sample, cache_k, cache_v, state_C, state_n, state_m, page_table, norm_mix, norm_mlp,
           moba_w_qkv, moba_q_gain, moba_k_gain, moba_w_out, mlstm_w_in, mlstm_b_gates, mlstm_h_gain,
           mlstm_w_out, mlp_w_up, mlp_w_down):
    batch, seq, dm = x_prompt.shape
    n_dec, t_dec, _ = x_sample.shape
    depth = norm_mix.shape[0]
    att_heads, att_dh = cache_k.shape[3], cache_k.shape[4]
    ml_heads, ml_dv, ml_dk = state_C.shape[2], state_C.shape[3], state_C.shape[4]
    n_prompt_rows, n_sample_rows = batch * seq, n_dec * t_dec
    tm_p = 512
    nb = seq // MOBA_BLOCK
    nbp = -(-nb // SUBLANES) * SUBLANES
    assert seq % max(MOBA_BLOCK, PROMPT_CHUNK, tm_p) == 0 and t_dec <= SAMPLE_CHUNK

    xp = x_prompt.reshape(n_prompt_rows, dm)
    xs = x_sample.reshape(n_sample_rows, dm)
    outs = {name: [] for name in ("kp", "vp", "cp", "np", "mp", "ks", "vs", "cs", "ns", "ms")}

    for i in range(depth):
        l = i // 2
        if i % 2 == 0:
            w_qkv = moba_w_qkv[l].astype(BF16)
            qp, kp, vp, kp_bf, vtp_bf, kmean = _moba_proj(xp, norm_mix[i], w_qkv, moba_q_gain[l], moba_k_gain[l],
                                                          att_heads, tm_p, True)
            kmean = jnp.pad(kmean.reshape(batch, nb, -1), ((0, 0), (0, nbp - nb), (0, 0)))
            ap = _moba_attn(qp, kp_bf, vtp_bf, kmean.reshape(batch * nbp, -1), batch, seq, att_heads)
            qs, ks, vs = _moba_proj(xs, norm_mix[i], w_qkv, moba_q_gain[l], moba_k_gain[l],
                                    att_heads, n_sample_rows, False)
            a_s = _moba_sample(qs, ks, vs, cache_k, cache_v, page_table, l, n_dec, att_heads)
            outs["kp"].append(kp); outs["vp"].append(vp); outs["ks"].append(ks); outs["vs"].append(vs)
            w_out = moba_w_out[l].astype(BF16)
        else:
            zc = jnp.zeros((batch, ml_heads, ml_dv, ml_dk), F32)
            zn = jnp.zeros((batch, ml_heads, ml_dk), F32)
            zm = jnp.zeros((batch, ml_heads, LANES), F32)
            q, k, vt, o, gt = _mlstm_proj(xp, norm_mix[i], mlstm_w_in[l], mlstm_b_gates[l], ml_heads, tm_p)
            ap, c_p, n_p, m_p = _mlstm_scan(q, k, vt, o, gt, mlstm_h_gain[l], zc, zn, zm,
                                            batch, PROMPT_CHUNK, ml_heads)
            q, k, vt, o, gt = _mlstm_proj(xs, norm_mix[i], mlstm_w_in[l], mlstm_b_gates[l], ml_heads, n_sample_rows)
            pad = lambda a: _pad_rows(a, n_dec, SAMPLE_CHUNK)
            pad_t = lambda a: _pad_rows(a.T, n_dec, SAMPLE_CHUNK).T
            idle = jnp.where(jnp.arange(LANES) < ml_heads, -jnp.inf, 0.0)
            real = (jnp.arange(n_dec * SAMPLE_CHUNK) % SAMPLE_CHUNK) < t_dec
            gt_pad = jnp.where(real[None, :], pad_t(gt), idle[:, None])
            m0 = jnp.broadcast_to(state_m[l][:, :, None], (n_dec, ml_heads, LANES))
            a_s, c_s, n_s, m_s = _mlstm_scan(pad(q), pad(k), pad_t(vt), pad(o), gt_pad, mlstm_h_gain[l],
                                             state_C[l], state_n[l], m0, n_dec, SAMPLE_CHUNK, ml_heads)
            a_s = a_s.reshape(n_dec, SAMPLE_CHUNK, -1)[:, :t_dec].reshape(n_sample_rows, -1)
            outs["cp"].append(c_p); outs["np"].append(n_p); outs["mp"].append(m_p[:, :, 0])
            outs["cs"].append(c_s); outs["ns"].append(n_s); outs["ms"].append(m_s[:, :, 0])
            w_out = mlstm_w_out[l].astype(BF16)
        w_up, w_down = mlp_w_up[i].astype(BF16), mlp_w_down[i].astype(BF16)
        xp = _mix_out_mlp(xp, ap, w_out, norm_mlp[i], w_up, w_down, tm_p)
        xs = _mix_out_mlp(xs, a_s, w_out, norm_mlp[i], w_up, w_down, n_sample_rows)

    kv_p = (len(outs["kp"]), batch, seq, att_heads, att_dh)
    kv_s = (len(outs["ks"]), n_dec, t_dec, att_heads, att_dh)
    return (xp.reshape(batch, seq, dm), xs.reshape(n_dec, t_dec, dm),
            jnp.stack(outs["kp"]).reshape(kv_p), jnp.stack(outs["vp"]).reshape(kv_p),
            jnp.stack(outs["cp"]), jnp.stack(outs["np"]), jnp.stack(outs["mp"]),
            jnp.stack(outs["ks"]).reshape(kv_s), jnp.stack(outs["vs"]).reshape(kv_s),
            jnp.stack(outs["cs"]), jnp.stack(outs["ns"]), jnp.stack(outs["ms"]))
```

```python
import functools
import math

import jax
import jax.numpy as jnp
from jax import lax
from jax.experimental import pallas as pl
from jax.experimental.pallas import tpu as pltpu

F32 = jnp.float32
BF16 = jnp.bfloat16
HIGHEST = lax.Precision.HIGHEST

RMS_EPS = 1e-6
MOBA_BLOCK = 256
MOBA_TOPK = 3
LANES = 128
SUBLANES = 8
BF16_ROWS = 16
SAMPLE_CHUNK = 128
PROMPT_CHUNK = 256
ATTN_HEADS_PER_STEP = 2
SAMPLE_BLOCKS_PER_STEP = 4
NEG = -1e30
FIXED_SHIFT_MAX = 60.0
VMEM_LIMIT = 56 * 1024 * 1024


def _params(*sem):
    return pltpu.CompilerParams(dimension_semantics=sem, vmem_limit_bytes=VMEM_LIMIT)


def _dot(a, b, precision=None):
    return jnp.dot(a, b, preferred_element_type=F32, precision=precision)


def _dot_nt(a, b, precision=None):
    return lax.dot_general(a, b, (((1,), (1,)), ((), ())), preferred_element_type=F32, precision=precision)


def _dot_tn(a, b):
    return lax.dot_general(a, b, (((0,), (0,)), ((), ())), preferred_element_type=F32)


def _rms(x, g):
    return x * lax.rsqrt(jnp.mean(x * x, axis=-1, keepdims=True) + RMS_EPS) * g


def _log_sigmoid(x):
    return -(jnp.maximum(-x, 0.0) + jnp.log1p(jnp.exp(-jnp.abs(x))))


def _resident(shape):
    return pl.BlockSpec(shape, lambda *_: (0,) * len(shape), pipeline_mode=pl.Buffered(1))


def _top_blocks(g, axis):
    idx = lax.broadcasted_iota(jnp.int32, g.shape, axis).astype(F32)
    sel = jnp.zeros(g.shape, F32)
    for _ in range(MOBA_TOPK):
        m = jnp.max(g, axis=axis, keepdims=True)
        first = jnp.min(jnp.where(g == m, idx, 1e9), axis=axis, keepdims=True)
        pick = idx == jnp.where(m > -jnp.inf, first, -1.0)
        sel = jnp.where(pick, 1.0, sel)
        g = jnp.where(pick, -jnp.inf, g)
    return sel


def _moba_proj_kernel(x_ref, g_ref, w_ref, qg_ref, kg_ref, q_ref, k_ref, v_ref,
                      kb_ref=None, vt_ref=None, km_ref=None, *, n_heads, dh):
    d = n_heads * dh
    xn = _rms(x_ref[...], g_ref[...])
    qkv = _dot(xn.astype(BF16), w_ref[...])
    for h in range(n_heads):
        sl = slice(h * dh, (h + 1) * dh)
        q_ref[:, sl] = _rms(qkv[:, h * dh:(h + 1) * dh], qg_ref[...])
        kn = _rms(qkv[:, d + h * dh:d + (h + 1) * dh], kg_ref[...])
        k_ref[:, sl] = kn
        if kb_ref is not None:
            kb_ref[:, sl] = kn.astype(BF16)
    v = qkv[:, 2 * d:]
    v_ref[...] = v
    if vt_ref is not None:
        vt_ref[...] = v.T.astype(BF16)
    if km_ref is not None:
        for r in range(km_ref.shape[0]):
            km_ref[r] = jnp.mean(k_ref[r * MOBA_BLOCK:(r + 1) * MOBA_BLOCK, :], axis=0, keepdims=True)


def _moba_proj(x, gain, w_bf, q_gain, k_gain, n_heads, tm, for_prompt):
    t, dm = x.shape
    d = w_bf.shape[1] // 3
    dh = d // n_heads
    row = lambda i: (i, 0)
    out_specs = [pl.BlockSpec((tm, d), row)] * 3
    out_shape = [jax.ShapeDtypeStruct((t, d), F32)] * 3
    if for_prompt:
        out_specs += [pl.BlockSpec((tm, d), row), pl.BlockSpec((d, tm), lambda i: (0, i)),
                      pl.BlockSpec((tm // MOBA_BLOCK, 1, d), lambda i: (i, 0, 0))]
        out_shape += [jax.ShapeDtypeStruct((t, d), BF16), jax.ShapeDtypeStruct((d, t), BF16),
                      jax.ShapeDtypeStruct((t // MOBA_BLOCK, 1, d), F32)]
    return pl.pallas_call(
        functools.partial(_moba_proj_kernel, n_heads=n_heads, dh=dh),
        grid=(t // tm,),
        in_specs=[pl.BlockSpec((tm, dm), row), _resident((1, dm)), _resident((dm, 3 * d)),
                  _resident((1, dh)), _resident((1, dh))],
        out_specs=out_specs,
        out_shape=out_shape,
        compiler_params=_params("parallel"),
        name="moba_proj",
    )(x, gain.reshape(1, dm), w_bf, q_gain.reshape(1, dh), k_gain.reshape(1, dh))


def _moba_attn_kernel(bound_ref, q_ref, k_ref, vt_ref, km_ref, o_ref, bias_ref, s_ref, *, dh, scale):
    blk = q_ref.shape[0]
    step = 2 * blk
    i = pl.program_id(2)
    heads = range(q_ref.shape[1] // dh)
    nbp = km_ref.shape[0]
    blk_i = lax.broadcasted_iota(jnp.int32, (nbp, blk), 0)
    causal =(lax.broadcasted_iota(jnp.int32, (blk, blk), 0) <= lax.broadcasted_iota(jnp.int32, (blk, blk), 1))
    start = pl.multiple_of(i * blk, blk)
    last_trip = k_ref.shape[0] // step - 1
    trips = (i + 1) // 2
    bound = bound_ref[0, 0]

    def scores(t, h, qb):
        st = pl.multiple_of(jnp.minimum(t, last_trip) * step, step)
        return _dot_nt(k_ref[pl.ds(st, step), h * dh:(h + 1) * dh], qb)

    def bias_rows(t, h):
        return jnp.concatenate(
            [jnp.broadcast_to(bias_ref[h, pl.ds(2 * t + u, 1), :], (blk, blk)) for u in range(2)], axis=0)

    qbs = []
    for h in heads:
        sl = slice(h * dh, (h + 1) * dh)
        q = q_ref[:, sl]
        gates = _dot_nt(km_ref[:, sl], q, HIGHEST)
        sel = _top_blocks(jnp.where(blk_i < i, gates, -jnp.inf), 0)
        bias_ref[h, :nbp, :] = jnp.where(sel > 0, 0.0, NEG)
        bias_ref[h, nbp:, :] = jnp.full((bias_ref.shape[1] - nbp, blk), NEG, F32)
        qbs.append((q * scale).astype(BF16))

    def own_block(h):
        sl = slice(h * dh, (h + 1) * dh)
        s = _dot_nt(k_ref[pl.ds(start, blk), sl], qbs[h])
        return jnp.where(causal, s, -jnp.inf), vt_ref[sl, pl.ds(start, blk)]

    def fixed_shift():
        state = []
        for h in heads:
            s, vt = own_block(h)
            p = jnp.exp2(s - bound)
            state.append((jnp.sum(p, axis=0, keepdims=True), _dot(vt, p.astype(BF16))))
            for u in range(2):
                s_ref[u, h] = scores(u, h, qbs[h])

        def body(tt, carry):
            out = list(carry)
            for u in range(2):
                t = 2 * tt + u
                st = pl.multiple_of(jnp.minimum(t, last_trip) * step, step)
                for h in heads:
                    l, acc = out[h]
                    p = jnp.exp2(s_ref[u, h] + (bias_rows(t, h) - bound))
                    s_ref[u, h] = scores(t + 2, h, qbs[h])
                    l = l + jnp.sum(p, axis=0, keepdims=True)
                    acc = acc + _dot(vt_ref[h * dh:(h + 1) * dh, pl.ds(st, step)], p.astype(BF16))
                    out[h] = (l, acc)
            return tuple(out)

        return lax.fori_loop(0, (trips + 1) // 2, body, tuple(state))

    def running_max():
        state = []
        for h in heads:
            s, vt = own_block(h)
            m = jnp.max(s, axis=0, keepdims=True)
            p = jnp.exp2(s - m)
            state.append((m, jnp.sum(p, axis=0, keepdims=True), _dot(vt, p.astype(BF16)), scores(0, h, qbs[h])))

        def body(t, carry):
            st = pl.multiple_of(t * step, step)
            out = []
            for h in heads:
                m, l, acc, s = carry[h]
                s_next = scores(t + 1, h, qbs[h])
                s = s + bias_rows(t, h)
                mn = jnp.maximum(m, jnp.max(s, axis=0, keepdims=True))
                alpha = jnp.exp2(m - mn)
                p = jnp.exp2(s - mn)
                l = alpha * l + jnp.sum(p, axis=0, keepdims=True)
                acc = alpha * acc + _dot(vt_ref[h * dh:(h + 1) * dh, pl.ds(st, step)], p.astype(BF16))
                out.append((mn, l, acc, s_next))
            return tuple(out)

        return tuple(c[1:3] for c in lax.fori_loop(0, trips, body, tuple(state)))

    final = lax.cond(bound <= FIXED_SHIFT_MAX, fixed_shift, running_max)
    for h in heads:
        l, acc = final[h]
        o_ref[:, h * dh:(h + 1) * dh] = (acc / l).T.astype(o_ref.dtype)


def _moba_attn(q, k_bf, vt_bf, k_mean, score_bound, batch, seq, n_heads):
    d = q.shape[1]
    dh = d // n_heads
    nb = seq // MOBA_BLOCK
    nbp = k_mean.shape[0] // batch
    hb = ATTN_HEADS_PER_STEP
    assert nb % 2 == 0 and n_heads % hb == 0
    return pl.pallas_call(
        functools.partial(_moba_attn_kernel, dh=dh, scale=dh ** -0.5 * math.log2(math.e)),
        grid=(batch, n_heads // hb, nb),
        in_specs=[pl.BlockSpec(memory_space=pltpu.SMEM),
                  pl.BlockSpec((MOBA_BLOCK, hb * dh), lambda b, h, i: (b * nb + i, h)),
                  pl.BlockSpec((seq, hb * dh), lambda b, h, i: (b, h)),
                  pl.BlockSpec((hb * dh, seq), lambda b, h, i: (h, b)),
                  pl.BlockSpec((nbp, hb * dh), lambda b, h, i: (b, h))],
        out_specs=pl.BlockSpec((MOBA_BLOCK, hb * dh), lambda b, h, i: (b * nb + i, h)),
        out_shape=jax.ShapeDtypeStruct((batch * seq, d), BF16),
        scratch_shapes=[pltpu.VMEM((hb, nbp + SUBLANES, MOBA_BLOCK), F32),
                        pltpu.VMEM((2, hb, 2 * MOBA_BLOCK, MOBA_BLOCK), F32)],
        compiler_params=_params("parallel", "parallel", "arbitrary"),
        name="moba_attn",
    )(score_bound, q, k_bf, vt_bf, k_mean)


def _sum_row_groups(x, rows):
    while x.shape[0] > rows:
        half = x.shape[0] // 2
        x = x[:half] + x[half:]
    return x


def _moba_sample_kernel(pt_ref, q_ref, kn_ref, vn_ref, *refs, n_heads, scale):
    del pt_ref
    bps = SAMPLE_BLOCKS_PER_STEP
    k_refs, v_refs = refs[:2 * bps], refs[2 * bps:4 * bps]
    o_ref, mask_ref, g_all, m_all, l_all, o_all = refs[4 * bps:]
    j = pl.program_id(1)
    r, dh = q_ref.shape
    t = r // n_heads
    rows = k_refs[0].shape[0]

    @pl.when(j == 0)
    def _():
        row_head = lax.broadcasted_iota(jnp.int32, (r, rows), 0) // t
        col_head = lax.rem(lax.broadcasted_iota(jnp.int32, (r, rows), 1), n_heads)
        mask_ref[...] = jnp.where(row_head == col_head, 0.0, -jnp.inf)

    q = q_ref[...]
    qb = q.astype(BF16)
    mask = mask_ref[...]
    for u in range(bps):
        k0 = k_refs[2 * u][...]
        k1 = k_refs[2 * u + 1][...]
        s0 = _dot_nt(qb, k0.astype(BF16)) * scale + mask
        s1 = _dot_nt(qb, k1.astype(BF16)) * scale + mask
        m = jnp.maximum(jnp.max(s0, axis=1, keepdims=True), jnp.max(s1, axis=1, keepdims=True))
        p0 = jnp.exp(s0 - m)
        p1 = jnp.exp(s1 - m)
        l = jnp.sum(p0, axis=1, keepdims=True) + jnp.sum(p1, axis=1, keepdims=True)
        o = (_dot(p0.astype(BF16), v_refs[2 * u][...].astype(BF16))
             + _dot(p1.astype(BF16), v_refs[2 * u + 1][...].astype(BF16)))
        k_mean = (_sum_row_groups(k0, n_heads) + _sum_row_groups(k1, n_heads)) * (n_heads / (2.0 * rows))
        k_mean = jnp.concatenate([jnp.broadcast_to(k_mean[h:h + 1, :], (t, dh)) for h in range(n_heads)], axis=0)
        gate = jnp.sum(q * k_mean, axis=1, keepdims=True)
        g_all[bps * j + u] = jnp.broadcast_to(gate, (r, LANES))
        m_all[bps * j + u] = jnp.broadcast_to(m, (r, LANES))
        l_all[bps * j + u] = jnp.broadcast_to(l, (r, LANES))
        o_all[bps * j + u] = o

    @pl.when(j == pl.num_programs(1) - 1)
    def _():
        pad = jnp.zeros((LANES - r, dh), F32)
        kp = jnp.concatenate([kn_ref[...], pad], axis=0).astype(BF16)
        vp = jnp.concatenate([vn_ref[...], pad], axis=0).astype(BF16)
        so = _dot_nt(qb, kp) * scale
        row = lax.broadcasted_iota(jnp.int32, so.shape, 0)
        col = lax.broadcasted_iota(jnp.int32, so.shape, 1)
        same_head = (row // t) == lax.rem(col, n_heads)
        causal = (col // n_heads) <= lax.rem(row, t)
        so = jnp.where(same_head, jnp.where(causal, so, -jnp.inf), -jnp.inf)
        m_own = jnp.max(so, axis=1, keepdims=True)
        po = jnp.exp(so - m_own)
        l_own = jnp.sum(po, axis=1, keepdims=True)
        o_own = _dot(po.astype(BF16), vp)

        sel = _top_blocks(g_all[...], 0)
        ma = m_all[...]
        mm = jnp.maximum(jnp.max(jnp.where(sel > 0, ma, -jnp.inf), axis=0), m_own)
        w = jnp.where(sel > 0, jnp.exp(ma - mm), 0.0)
        w_own = jnp.exp(m_own - mm)
        den = jnp.sum(w * l_all[...], axis=0) + w_own * l_own
        num = jnp.sum(w * o_all[...], axis=0) + w_own * o_own
        o_ref[...] = num / den


def _moba_sample(q, k_new, v_new, cache_k, cache_v, page_table, layer, n_seq, n_heads):
    n_layers, pool, page, _, dh = cache_k.shape
    t = q.shape[0] // n_seq
    r = n_heads * t
    rows = page * n_heads
    bps = SAMPLE_BLOCKS_PER_STEP
    assert MOBA_BLOCK == 2 * page and r <= LANES and dh == LANES
    nblk = page_table.shape[1] // 2
    assert nblk >= MOBA_TOPK and nblk % bps == 0
    ck = cache_k.reshape(n_layers, pool, rows, dh)
    cv = cache_v.reshape(n_layers, pool, rows, dh)
    q_ht = q.reshape(n_seq, t, n_heads, dh).transpose(0, 2, 1, 3).reshape(n_seq * r, dh)
    tok = pl.BlockSpec((r, dh), lambda s, j, pt: (s, 0))
    pg = lambda o: pl.BlockSpec((None, None, rows, dh), lambda s, j, pt: (layer, pt[s, 2 * bps * j + o], 0, 0))
    pages = [pg(o) for o in range(2 * bps)]
    stat = pltpu.VMEM((nblk, r, LANES), F32)
    out = pl.pallas_call(
        functools.partial(_moba_sample_kernel, n_heads=n_heads, scale=dh ** -0.5),
        grid_spec=pltpu.PrefetchScalarGridSpec(
            num_scalar_prefetch=1,
            grid=(n_seq, nblk // bps),
            in_specs=[tok, tok, tok] + pages + pages,
            out_specs=tok,
            scratch_shapes=[pltpu.VMEM((r, rows), F32), stat, stat, stat, stat]),
        out_shape=jax.ShapeDtypeStruct((n_seq * r, dh), F32),
        compiler_params=_params("parallel", "arbitrary"),
        name="moba_sample",
    )(page_table, q_ht, k_new.reshape(n_seq * r, dh), v_new.reshape(n_seq * r, dh),
      *([ck] * (2 * bps)), *([cv] * (2 * bps)))
    return out.reshape(n_seq, n_heads, t, dh).transpose(0, 2, 1, 3).reshape(n_seq * t, n_heads * dh)


def _mlstm_proj_kernel(x_ref, g_ref, w_ref, wg_ref, bg_ref, q_ref, k_ref, vt_ref, o_ref, gt_ref, *, n_heads, hk, hv):
    xn = _rms(x_ref[...], g_ref[...])
    z = _dot(xn.astype(BF16), w_ref[...])
    dk = hk // n_heads
    q_ref[...] = z[:, :hk].astype(BF16)
    k_ref[...] = (z[:, hk:2 * hk] * dk ** -0.5).astype(BF16)
    vt_ref[...] = z[:, 2 * hk:2 * hk + hv].T.astype(BF16)
    o_ref[...] = jax.nn.sigmoid(z[:, 2 * hk + hv:])
    g = _dot(xn, wg_ref[...], HIGHEST) + bg_ref[...]
    c = lax.broadcasted_iota(jnp.int32, g.shape, 1)
    g = jnp.where((c >= n_heads) & (c < 2 * n_heads), _log_sigmoid(g), g)
    gt_ref[...] = g.T


def _mlstm_proj(x, gain, w_in, b_gates, n_heads, tm):
    t, dm = x.shape
    ng = 2 * n_heads
    hv = dm
    hk = (w_in.shape[1] - ng - 2 * hv) // 2
    w_main = w_in[:, :2 * hk + 2 * hv].astype(BF16)
    w_g = jnp.pad(w_in[:, 2 * hk + 2 * hv:], ((0, 0), (0, LANES - ng)))
    b_g = jnp.pad(b_gates, (0, LANES - ng))
    row = lambda i: (i, 0)
    return pl.pallas_call(
        functools.partial(_mlstm_proj_kernel, n_heads=n_heads, hk=hk, hv=hv),
        grid=(t // tm,),
        in_specs=[pl.BlockSpec((tm, dm), row), _resident((1, dm)), _resident((dm, 2 * hk + 2 * hv)),
                  _resident((dm, LANES)), _resident((1, LANES))],
        out_specs=[pl.BlockSpec((tm, hk), row), pl.BlockSpec((tm, hk), row), pl.BlockSpec((hv, tm), lambda i: (0, i)),
                   pl.BlockSpec((tm, hv), row), pl.BlockSpec((LANES, tm), lambda i: (0, i))],
        out_shape=[jax.ShapeDtypeStruct((t, hk), BF16), jax.ShapeDtypeStruct((t, hk), BF16),
                   jax.ShapeDtypeStruct((hv, t), BF16), jax.ShapeDtypeStruct((t, hv), F32),
                   jax.ShapeDtypeStruct((LANES, t), F32)],
        compiler_params=_params("parallel"),
        name="mlstm_proj",
    )(x, gain.reshape(1, dm), w_main, w_g, b_g.reshape(1, LANES))


def _scan_lanes(x, op, fill):
    lane = lax.broadcasted_iota(jnp.int32, x.shape, 1)
    d = 1
    while d < x.shape[1]:
        x = op(x, jnp.where(lane >= d, pltpu.roll(x, d, 1), fill))
        d *= 2
    return x


def _mlstm_kernel(q_ref, k_ref, vt_ref, o_ref, gt_ref, hg_ref, c0_ref, n0_ref, m0_ref,
                  y_ref, c_ref, n_ref, m_ref, *, n_heads):
    @pl.when(pl.program_id(1) == 0)
    def _():
        c_ref[...] = c0_ref[...]
        n_ref[...] = n0_ref[...]
        m_ref[...] = m0_ref[...]

    ln = q_ref.shape[0]
    dk = q_ref.shape[1] // n_heads
    dv = vt_ref.shape[0] // n_heads
    gt = gt_ref[...]
    ig, fg = gt[:n_heads], gt[n_heads:2 * n_heads]
    bcum = _scan_lanes(fg, jnp.add, 0.0)
    a = ig - bcum
    m_prev = m_ref[0][:, :1]
    mx = jnp.maximum(m_prev, _scan_lanes(a, jnp.maximum, -jnp.inf))
    mx_last = mx[:, ln - 1:]
    inter = jnp.exp(m_prev - mx)
    floor = jnp.exp(-(bcum + mx))
    decay = jnp.exp(a - mx_last)
    carry = jnp.exp(m_prev - mx_last)
    a_cols = jnp.concatenate([a, jnp.zeros((LANES - n_heads, ln), F32)], axis=0).T
    si = lax.broadcasted_iota(jnp.int32, (ln, ln), 0)
    ti = lax.broadcasted_iota(jnp.int32, (ln, ln), 1)
    visible = si <= ti
    ones = jnp.ones((BF16_ROWS, ln), BF16)
    for h in range(n_heads):
        row = slice(h, h + 1)
        qh = q_ref[:, h * dk:(h + 1) * dk]
        kh = k_ref[:, h * dk:(h + 1) * dk]
        vth = vt_ref[h * dv:(h + 1) * dv, :]
        c_prev = c_ref[0, h]
        n_prev = n_ref[0, row, :]
        dmat = jnp.exp(jnp.where(visible, a_cols[:, row] - mx[row], -jnp.inf))
        w = dmat * _dot_nt(kh, qh)
        c_aug = jnp.concatenate([c_prev, jnp.broadcast_to(n_prev, (BF16_ROWS, dk))], axis=0).astype(BF16)
        v_aug = jnp.concatenate([vth, ones], axis=0)
        x = inter[row] * _dot_nt(c_aug, qh) + _dot(v_aug, w.astype(BF16))
        hh = x[:dv] / jnp.maximum(jnp.abs(x[dv:dv + 1]), floor[row])
        hn = hh * lax.rsqrt(jnp.mean(hh * hh, axis=0, keepdims=True) + RMS_EPS)
        y = hn.T * hg_ref[:, h * dv:(h + 1) * dv] * o_ref[:, h * dv:(h + 1) * dv]
        y_ref[:, h * dv:(h + 1) * dv] = y.astype(y_ref.dtype)
        vd = jnp.concatenate([vth.astype(F32) * decay[row], jnp.broadcast_to(decay[row], (BF16_ROWS, ln))], axis=0)
        upd = _dot(vd.astype(BF16), kh)
        c_ref[0, h] = carry[row] * c_prev + upd[:dv]
        n_ref[0, row, :] = carry[row] * n_prev + upd[dv:dv + 1]
    m_ref[0] = jnp.broadcast_to(bcum[:, ln - 1:] + mx_last, (n_heads, LANES))


def _mlstm_scan(q, k, vt, o, gates_t, h_gain, c0, n0, m0, n_seq, chunk, n_heads):
    t, hk = q.shape
    hv = vt.shape[0]
    nc = t // n_seq // chunk
    row = lambda s, c: (s * nc + c, 0)
    col = lambda s, c: (0, s * nc + c)
    st4 = pl.BlockSpec((1,) + c0.shape[1:], lambda s, c: (s, 0, 0, 0))
    st3n = pl.BlockSpec((1,) + n0.shape[1:], lambda s, c: (s, 0, 0))
    st3m = pl.BlockSpec((1,) + m0.shape[1:], lambda s, c: (s, 0, 0))
    return pl.pallas_call(
        functools.partial(_mlstm_kernel, n_heads=n_heads),
        grid=(n_seq, nc),
        in_specs=[pl.BlockSpec((chunk, hk), row), pl.BlockSpec((chunk, hk), row), pl.BlockSpec((hv, chunk), col),
                  pl.BlockSpec((chunk, hv), row), pl.BlockSpec((LANES, chunk), col),
                  pl.BlockSpec((1, hv), lambda s, c: (0, 0)), st4, st3n, st3m],
        out_specs=[pl.BlockSpec((chunk, hv), row), st4, st3n, st3m],
        out_shape=[jax.ShapeDtypeStruct((t, hv), BF16), jax.ShapeDtypeStruct(c0.shape, F32),
                   jax.ShapeDtypeStruct(n0.shape, F32), jax.ShapeDtypeStruct(m0.shape, F32)],
        compiler_params=_params("parallel", "arbitrary"),
        name="mlstm_scan",
    )(q, k, vt, o, gates_t, h_gain.reshape(1, hv), c0, n0, m0)


def _pad_rows(a, n_seq, chunk):
    t = a.shape[0] // n_seq
    a = a.reshape(n_seq, t, a.shape[1])
    a = jnp.pad(a, ((0, 0), (0, chunk - t), (0, 0)))
    return a.reshape(n_seq * chunk, a.shape[2])


def _mlp_kernel(x_ref, a_ref, wo_ref, g_ref, wu_ref, wd_ref, out_ref, *, ff_chunk):
    x1 = x_ref[...] + _dot(a_ref[...].astype(BF16), wo_ref[...])
    xn = _rms(x1, g_ref[...]).astype(BF16)
    acc = x1
    for c in range(wu_ref.shape[1] // ff_chunk):
        hid = _dot(xn, wu_ref[:, c * ff_chunk:(c + 1) * ff_chunk])
        hid = jnp.square(jnp.maximum(hid, 0.0)).astype(BF16)
        acc = acc + _dot(hid, wd_ref[c * ff_chunk:(c + 1) * ff_chunk, :])
    out_ref[...] = acc


def _mix_out_mlp(x, a, w_out_bf, gain, w_up_bf, w_down_bf, tm):
    t, dm = x.shape
    dff = w_up_bf.shape[1]
    row = lambda i: (i, 0)
    return pl.pallas_call(
        functools.partial(_mlp_kernel, ff_chunk=1024),
        grid=(t // tm,),
        in_specs=[pl.BlockSpec((tm, dm), row), pl.BlockSpec((tm, a.shape[1]), row), _resident(w_out_bf.shape),
                  _resident((1, dm)), _resident((dm, dff)), _resident((dff, dm))],
        out_specs=pl.BlockSpec((tm, dm), row),
        out_shape=jax.ShapeDtypeStruct((t, dm), F32),
        compiler_params=_params("parallel"),
        name="mix_out_mlp",
    )(x, a, w_out_bf, gain.reshape(1, dm), w_up_bf, w_down_bf)


def kernel(x_prompt, x_sample, cache_k, cache_v, state_C, state_n, state_m, page_table, norm_mix, norm_mlp,
           moba_w_qkv, moba_q_gain, moba_k_gain, moba_w_out, mlstm_w_in, mlstm_b_gates, mlstm_h_gain,
           mlstm_w_out, mlp_w_up, mlp_w_down):
    batch, seq, dm = x_prompt.shape
    n_dec, t_dec, _ = x_sample.shape
    depth = norm_mix.shape[0]
    att_heads, att_dh = cache_k.shape[3], cache_k.shape[4]
    ml_heads, ml_dv, ml_dk = state_C.shape[2], state_C.shape[3], state_C.shape[4]
    n_prompt_rows, n_sample_rows = batch * seq, n_dec * t_dec
    tm_p = 512
    nb = seq // MOBA_BLOCK
    nbp = -(-nb // SUBLANES) * SUBLANES
    assert seq % max(MOBA_BLOCK, PROMPT_CHUNK, tm_p) == 0 and t_dec <= SAMPLE_CHUNK

    xp = x_prompt.reshape(n_prompt_rows, dm)
    xs = x_sample.reshape(n_sample_rows, dm)
    outs = {name: [] for name in ("kp", "vp", "cp", "np", "mp", "ks", "vs", "cs", "ns", "ms")}

    for i in range(depth):
        l = i // 2
        if i % 2 == 0:
            w_qkv = moba_w_qkv[l].astype(BF16)
            qp, kp, vp, kp_bf, vtp_bf, kmean = _moba_proj(xp, norm_mix[i], w_qkv, moba_q_gain[l], moba_k_gain[l],
                                                          att_heads, tm_p, True)
            kmean = jnp.pad(kmean.reshape(batch, nb, -1), ((0, 0), (0, nbp - nb), (0, 0)))
            bound = (1.01 * att_dh ** 0.5 * math.log2(math.e)
                     * jnp.max(jnp.abs(moba_q_gain[l])) * jnp.max(jnp.abs(moba_k_gain[l]))).reshape(1, 1)
            ap = _moba_attn(qp, kp_bf, vtp_bf, kmean.reshape(batch * nbp, -1), bound, batch, seq, att_heads)
            qs, ks, vs = _moba_proj(xs, norm_mix[i], w_qkv, moba_q_gain[l], moba_k_gain[l],
                                    att_heads, n_sample_rows, False)
            a_s = _moba_sample(qs, ks, vs, cache_k, cache_v, page_table, l, n_dec, att_heads)
            outs["kp"].append(kp); outs["vp"].append(vp); outs["ks"].append(ks); outs["vs"].append(vs)
            w_out = moba_w_out[l].astype(BF16)
        else:
            zc = jnp.zeros((batch, ml_heads, ml_dv, ml_dk), F32)
            zn = jnp.zeros((batch, ml_heads, ml_dk), F32)
            zm = jnp.zeros((batch, ml_heads, LANES), F32)
            q, k, vt, o, gt = _mlstm_proj(xp, norm_mix[i], mlstm_w_in[l], mlstm_b_gates[l], ml_heads, tm_p)
            ap, c_p, n_p, m_p = _mlstm_scan(q, k, vt, o, gt, mlstm_h_gain[l], zc, zn, zm,
                                            batch, PROMPT_CHUNK, ml_heads)
            q, k, vt, o, gt = _mlstm_proj(xs, norm_mix[i], mlstm_w_in[l], mlstm_b_gates[l], ml_heads, n_sample_rows)
            pad = lambda a: _pad_rows(a, n_dec, SAMPLE_CHUNK)
            pad_t = lambda a: _pad_rows(a.T, n_dec, SAMPLE_CHUNK).T
            idle = jnp.where(jnp.arange(LANES) < ml_heads, -jnp.inf, 0.0)
            real = (jnp.arange(n_dec * SAMPLE_CHUNK) % SAMPLE_CHUNK) < t_dec
            gt_pad = jnp.where(real[None, :], pad_t(gt), idle[:, None])
            m0 = jnp.broadcast_to(state_m[l][:, :, None], (n_dec, ml_heads, LANES))
            a_s, c_s, n_s, m_s = _mlstm_scan(pad(q), pad(k), pad_t(vt), pad(o), gt_pad, mlstm_h_gain[l],
                                             state_C[l], state_n[l], m0, n_dec, SAMPLE_CHUNK, ml_heads)
            a_s = a_s.reshape(n_dec, SAMPLE_CHUNK, -1)[:, :t_dec].reshape(n_sample_rows, -1)
            outs["cp"].append(c_p); outs["np"].append(n_p); outs["mp"].append(m_p[:, :, 0])
            outs["cs"].append(c_s); outs["ns"].append(n_s); outs["ms"].append(m_s[:, :, 0])
            w_out = mlstm_w_out[l].astype(BF16)
        w_up, w_down = mlp_w_up[i].astype(BF16), mlp_w_down[i].astype(BF16)
        xp = _mix_out_mlp(xp, ap, w_out, norm_mlp[i], w_up, w_down, tm_p)
        xs = _mix_out_mlp(xs, a_s, w_out, norm_mlp[i], w_up, w_down, n_sample_rows)

    kv_p = (len(outs["kp"]), batch, seq, att_heads, att_dh)
    kv_s = (len(outs["ks"]), n_dec, t_dec, att_heads, att_dh)
    return (xp.reshape(batch, seq, dm), xs.reshape(n_dec, t_dec, dm),
            jnp.stack(outs["kp"]).reshape(kv_p), jnp.stack(outs["vp"]).reshape(kv_p),
            jnp.stack(outs["cp"]), jnp.stack(outs["np"]), jnp.stack(outs["mp"]),
            jnp.stack(outs["ks"]).reshape(kv_s), jnp.stack(outs["vs"]).reshape(kv_s),
            jnp.stack(outs["cs"]), jnp.stack(outs["ns"]), jnp.stack(outs["ms"]))
```

```python
import functools
import math

import jax
import jax.numpy as jnp
from jax import lax
from jax.experimental import pallas as pl
from jax.experimental.pallas import tpu as pltpu

F32 = jnp.float32
BF16 = jnp.bfloat16
HIGHEST = lax.Precision.HIGHEST

RMS_EPS = 1e-6
MOBA_BLOCK = 256
MOBA_TOPK = 3
LANES = 128
SUBLANES = 8
BF16_ROWS = 16
SAMPLE_CHUNK = 128
PROMPT_CHUNK = 256
ATTN_HEADS_PER_STEP = 4
SAMPLE_BLOCKS_PER_STEP = 4
NEG = -1e30
FIXED_SHIFT_MAX = 60.0
VMEM_LIMIT = 56 * 1024 * 1024


def _params(*sem):
    return pltpu.CompilerParams(dimension_semantics=sem, vmem_limit_bytes=VMEM_LIMIT)


def _dot(a, b, precision=None):
    return jnp.dot(a, b, preferred_element_type=F32, precision=precision)


def _dot_nt(a, b, precision=None):
    return lax.dot_general(a, b, (((1,), (1,)), ((), ())), preferred_element_type=F32, precision=precision)


def _dot_tn(a, b):
    return lax.dot_general(a, b, (((0,), (0,)), ((), ())), preferred_element_type=F32)


def _rms(x, g):
    return x * lax.rsqrt(jnp.mean(x * x, axis=-1, keepdims=True) + RMS_EPS) * g


def _log_sigmoid(x):
    return -(jnp.maximum(-x, 0.0) + jnp.log1p(jnp.exp(-jnp.abs(x))))


def _resident(shape):
    return pl.BlockSpec(shape, lambda *_: (0,) * len(shape), pipeline_mode=pl.Buffered(1))


def _top_blocks(g, axis):
    idx = lax.broadcasted_iota(jnp.int32, g.shape, axis).astype(F32)
    sel = jnp.zeros(g.shape, F32)
    for _ in range(MOBA_TOPK):
        m = jnp.max(g, axis=axis, keepdims=True)
        first = jnp.min(jnp.where(g == m, idx, 1e9), axis=axis, keepdims=True)
        pick = idx == jnp.where(m > -jnp.inf, first, -1.0)
        sel = jnp.where(pick, 1.0, sel)
        g = jnp.where(pick, -jnp.inf, g)
    return sel


def _moba_proj_kernel(x_ref, g_ref, w_ref, qg_ref, kg_ref, q_ref, k_ref, v_ref,
                      kb_ref=None, vt_ref=None, km_ref=None, *, n_heads, dh):
    d = n_heads * dh
    xn = _rms(x_ref[...], g_ref[...])
    qkv = _dot(xn.astype(BF16), w_ref[...])
    for h in range(n_heads):
        sl = slice(h * dh, (h + 1) * dh)
        q_ref[:, sl] = _rms(qkv[:, h * dh:(h + 1) * dh], qg_ref[...])
        kn = _rms(qkv[:, d + h * dh:d + (h + 1) * dh], kg_ref[...])
        k_ref[:, sl] = kn
        if kb_ref is not None:
            kb_ref[:, sl] = kn.astype(BF16)
    v = qkv[:, 2 * d:]
    v_ref[...] = v
    if vt_ref is not None:
        vt_ref[...] = v.T.astype(BF16)
    if km_ref is not None:
        for r in range(km_ref.shape[0]):
            km_ref[r] = jnp.mean(k_ref[r * MOBA_BLOCK:(r + 1) * MOBA_BLOCK, :], axis=0, keepdims=True)


def _moba_proj(x, gain, w_bf, q_gain, k_gain, n_heads, tm, for_prompt):
    t, dm = x.shape
    d = w_bf.shape[1] // 3
    dh = d // n_heads
    row = lambda i: (i, 0)
    out_specs = [pl.BlockSpec((tm, d), row)] * 3
    out_shape = [jax.ShapeDtypeStruct((t, d), F32)] * 3
    if for_prompt:
        out_specs += [pl.BlockSpec((tm, d), row), pl.BlockSpec((d, tm), lambda i: (0, i)),
                      pl.BlockSpec((tm // MOBA_BLOCK, 1, d), lambda i: (i, 0, 0))]
        out_shape += [jax.ShapeDtypeStruct((t, d), BF16), jax.ShapeDtypeStruct((d, t), BF16),
                      jax.ShapeDtypeStruct((t // MOBA_BLOCK, 1, d), F32)]
    return pl.pallas_call(
        functools.partial(_moba_proj_kernel, n_heads=n_heads, dh=dh),
        grid=(t // tm,),
        in_specs=[pl.BlockSpec((tm, dm), row), _resident((1, dm)), _resident((dm, 3 * d)),
                  _resident((1, dh)), _resident((1, dh))],
        out_specs=out_specs,
        out_shape=out_shape,
        compiler_params=_params("parallel"),
        name="moba_proj",
    )(x, gain.reshape(1, dm), w_bf, q_gain.reshape(1, dh), k_gain.reshape(1, dh))


def _moba_attn_kernel(bound_ref, q_ref, k_ref, vt_ref, km_ref, o_ref, bias_ref, s_ref, *, dh, scale):
    blk = q_ref.shape[0]
    step = 2 * blk
    i = pl.program_id(2)
    heads = range(q_ref.shape[1] // dh)
    nbp = km_ref.shape[0]
    blk_i = lax.broadcasted_iota(jnp.int32, (nbp, blk), 0)
    causal =(lax.broadcasted_iota(jnp.int32, (blk, blk), 0) <= lax.broadcasted_iota(jnp.int32, (blk, blk), 1))
    start = pl.multiple_of(i * blk, blk)
    last_trip = k_ref.shape[0] // step - 1
    trips = (i + 1) // 2
    bound = bound_ref[0, 0]

    def scores(t, h, qb):
        st = pl.multiple_of(jnp.minimum(t, last_trip) * step, step)
        return _dot_nt(k_ref[pl.ds(st, step), h * dh:(h + 1) * dh], qb)

    def bias_rows(t, h):
        return jnp.concatenate(
            [jnp.broadcast_to(bias_ref[h, pl.ds(2 * t + u, 1), :], (blk, blk)) for u in range(2)], axis=0)

    qbs = []
    for h in heads:
        sl = slice(h * dh, (h + 1) * dh)
        q = q_ref[:, sl]
        gates = _dot_nt(km_ref[:, sl], q, HIGHEST)
        sel = _top_blocks(jnp.where(blk_i < i, gates, -jnp.inf), 0)
        bias_ref[h, :nbp, :] = jnp.where(sel > 0, 0.0, NEG)
        bias_ref[h, nbp:, :] = jnp.full((bias_ref.shape[1] - nbp, blk), NEG, F32)
        qbs.append((q * scale).astype(BF16))

    def own_block(h):
        sl = slice(h * dh, (h + 1) * dh)
        s = _dot_nt(k_ref[pl.ds(start, blk), sl], qbs[h])
        return jnp.where(causal, s, -jnp.inf), vt_ref[sl, pl.ds(start, blk)]

    def fixed_shift():
        state = []
        for h in heads:
            s, vt = own_block(h)
            p = jnp.exp2(s - bound)
            state.append((jnp.sum(p, axis=0, keepdims=True), _dot(vt, p.astype(BF16))))
            for u in range(2):
                s_ref[u, h] = scores(u, h, qbs[h])

        def body(tt, carry):
            out = list(carry)
            for u in range(2):
                t = 2 * tt + u
                st = pl.multiple_of(jnp.minimum(t, last_trip) * step, step)
                for h in heads:
                    l, acc = out[h]
                    p = jnp.exp2(s_ref[u, h] + (bias_rows(t, h) - bound))
                    s_ref[u, h] = scores(t + 2, h, qbs[h])
                    l = l + jnp.sum(p, axis=0, keepdims=True)
                    acc = acc + _dot(vt_ref[h * dh:(h + 1) * dh, pl.ds(st, step)], p.astype(BF16))
                    out[h] = (l, acc)
            return tuple(out)

        return lax.fori_loop(0, (trips + 1) // 2, body, tuple(state))

    def running_max():
        state = []
        for h in heads:
            s, vt = own_block(h)
            m = jnp.max(s, axis=0, keepdims=True)
            p = jnp.exp2(s - m)
            state.append((m, jnp.sum(p, axis=0, keepdims=True), _dot(vt, p.astype(BF16)), scores(0, h, qbs[h])))

        def body(t, carry):
            st = pl.multiple_of(t * step, step)
            out = []
            for h in heads:
                m, l, acc, s = carry[h]
                s_next = scores(t + 1, h, qbs[h])
                s = s + bias_rows(t, h)
                mn = jnp.maximum(m, jnp.max(s, axis=0, keepdims=True))
                alpha = jnp.exp2(m - mn)
                p = jnp.exp2(s - mn)
                l = alpha * l + jnp.sum(p, axis=0, keepdims=True)
                acc = alpha * acc + _dot(vt_ref[h * dh:(h + 1) * dh, pl.ds(st, step)], p.astype(BF16))
                out.append((mn, l, acc, s_next))
            return tuple(out)

        return tuple(c[1:3] for c in lax.fori_loop(0, trips, body, tuple(state)))

    final = lax.cond(bound <= FIXED_SHIFT_MAX, fixed_shift, running_max)
    for h in heads:
        l, acc = final[h]
        o_ref[:, h * dh:(h + 1) * dh] = (acc / l).T.astype(o_ref.dtype)


def _moba_attn(q, k_bf, vt_bf, k_mean, score_bound, batch, seq, n_heads):
    d = q.shape[1]
    dh = d // n_heads
    nb = seq // MOBA_BLOCK
    nbp = k_mean.shape[0] // batch
    hb = ATTN_HEADS_PER_STEP
    assert nb % 2 == 0 and n_heads % hb == 0
    return pl.pallas_call(
        functools.partial(_moba_attn_kernel, dh=dh, scale=dh ** -0.5 * math.log2(math.e)),
        grid=(batch, n_heads // hb, nb),
        in_specs=[pl.BlockSpec(memory_space=pltpu.SMEM),
                  pl.BlockSpec((MOBA_BLOCK, hb * dh), lambda b, h, i: (b * nb + i, h)),
                  pl.BlockSpec((seq, hb * dh), lambda b, h, i: (b, h)),
                  pl.BlockSpec((hb * dh, seq), lambda b, h, i: (h, b)),
                  pl.BlockSpec((nbp, hb * dh), lambda b, h, i: (b, h))],
        out_specs=pl.BlockSpec((MOBA_BLOCK, hb * dh), lambda b, h, i: (b * nb + i, h)),
        out_shape=jax.ShapeDtypeStruct((batch * seq, d), BF16),
        scratch_shapes=[pltpu.VMEM((hb, nbp + SUBLANES, MOBA_BLOCK), F32),
                        pltpu.VMEM((2, hb, 2 * MOBA_BLOCK, MOBA_BLOCK), F32)],
        compiler_params=_params("parallel", "parallel", "arbitrary"),
        name="moba_attn",
    )(score_bound, q, k_bf, vt_bf, k_mean)


def _sum_row_groups(x, rows):
    while x.shape[0] > rows:
        half = x.shape[0] // 2
        x = x[:half] + x[half:]
    return x


def _moba_sample_kernel(pt_ref, q_ref, kn_ref, vn_ref, *refs, n_heads, scale):
    del pt_ref
    bps = SAMPLE_BLOCKS_PER_STEP
    k_refs, v_refs = refs[:2 * bps], refs[2 * bps:4 * bps]
    o_ref, mask_ref, g_all, m_all, l_all, o_all = refs[4 * bps:]
    j = pl.program_id(1)
    r, dh = q_ref.shape
    t = r // n_heads
    rows = k_refs[0].shape[0]

    @pl.when(j == 0)
    def _():
        row_head = lax.broadcasted_iota(jnp.int32, (r, rows), 0) // t
        col_head = lax.rem(lax.broadcasted_iota(jnp.int32, (r, rows), 1), n_heads)
        mask_ref[...] = jnp.where(row_head == col_head, 0.0, -jnp.inf)

    q = q_ref[...]
    qb = (q * scale).astype(BF16)
    mask = mask_ref[...]
    for u in range(bps):
        k0 = k_refs[2 * u][...]
        k1 = k_refs[2 * u + 1][...]
        s0 = _dot_nt(qb, k0.astype(BF16)) + mask
        s1 = _dot_nt(qb, k1.astype(BF16)) + mask
        m = jnp.maximum(jnp.max(s0, axis=1, keepdims=True), jnp.max(s1, axis=1, keepdims=True))
        p0 = jnp.exp2(s0 - m)
        p1 = jnp.exp2(s1 - m)
        l = jnp.sum(p0, axis=1, keepdims=True) + jnp.sum(p1, axis=1, keepdims=True)
        o = (_dot(p0.astype(BF16), v_refs[2 * u][...].astype(BF16))
             + _dot(p1.astype(BF16), v_refs[2 * u + 1][...].astype(BF16)))
        k_mean = (_sum_row_groups(k0, n_heads) + _sum_row_groups(k1, n_heads)) * (n_heads / (2.0 * rows))
        k_mean = jnp.concatenate([jnp.broadcast_to(k_mean[h:h + 1, :], (t, dh)) for h in range(n_heads)], axis=0)
        gate = jnp.sum(q * k_mean, axis=1, keepdims=True)
        g_all[bps * j + u] = jnp.broadcast_to(gate, (r, LANES))
        m_all[bps * j + u] = jnp.broadcast_to(m, (r, LANES))
        l_all[bps * j + u] = jnp.broadcast_to(l, (r, LANES))
        o_all[bps * j + u] = o

    @pl.when(j == pl.num_programs(1) - 1)
    def _():
        pad = jnp.zeros((LANES - r, dh), F32)
        kp = jnp.concatenate([kn_ref[...], pad], axis=0).astype(BF16)
        vp = jnp.concatenate([vn_ref[...], pad], axis=0).astype(BF16)
        so = _dot_nt(qb, kp)
        row = lax.broadcasted_iota(jnp.int32, so.shape, 0)
        col = lax.broadcasted_iota(jnp.int32, so.shape, 1)
        same_head = (row // t) == lax.rem(col, n_heads)
        causal = (col // n_heads) <= lax.rem(row, t)
        so = jnp.where(same_head, jnp.where(causal, so, -jnp.inf), -jnp.inf)
        m_own = jnp.max(so, axis=1, keepdims=True)
        po = jnp.exp2(so - m_own)
        l_own = jnp.sum(po, axis=1, keepdims=True)
        o_own = _dot(po.astype(BF16), vp)

        sel = _top_blocks(g_all[...], 0)
        ma = m_all[...]
        mm = jnp.maximum(jnp.max(jnp.where(sel > 0, ma, -jnp.inf), axis=0), m_own)
        w = jnp.where(sel > 0, jnp.exp2(ma - mm), 0.0)
        w_own = jnp.exp2(m_own - mm)
        den = jnp.sum(w * l_all[...], axis=0) + w_own * l_own
        num = jnp.sum(w * o_all[...], axis=0) + w_own * o_own
        o_ref[...] = num / den


def _moba_sample(q, k_new, v_new, cache_k, cache_v, page_table, layer, n_seq, n_heads):
    n_layers, pool, page, _, dh = cache_k.shape
    t = q.shape[0] // n_seq
    r = n_heads * t
    rows = page * n_heads
    bps = SAMPLE_BLOCKS_PER_STEP
    assert MOBA_BLOCK == 2 * page and r <= LANES and dh == LANES
    nblk = page_table.shape[1] // 2
    assert nblk >= MOBA_TOPK and nblk % bps == 0
    ck = cache_k.reshape(n_layers, pool, rows, dh)
    cv = cache_v.reshape(n_layers, pool, rows, dh)
    q_ht = q.reshape(n_seq, t, n_heads, dh).transpose(0, 2, 1, 3).reshape(n_seq * r, dh)
    tok = pl.BlockSpec((r, dh), lambda s, j, pt: (s, 0))
    pg = lambda o: pl.BlockSpec((None, None, rows, dh), lambda s, j, pt: (layer, pt[s, 2 * bps * j + o], 0, 0))
    pages = [pg(o) for o in range(2 * bps)]
    stat = pltpu.VMEM((nblk, r, LANES), F32)
    out = pl.pallas_call(
        functools.partial(_moba_sample_kernel, n_heads=n_heads, scale=dh ** -0.5 * math.log2(math.e)),
        grid_spec=pltpu.PrefetchScalarGridSpec(
            num_scalar_prefetch=1,
            grid=(n_seq, nblk // bps),
            in_specs=[tok, tok, tok] + pages + pages,
            out_specs=tok,
            scratch_shapes=[pltpu.VMEM((r, rows), F32), stat, stat, stat, stat]),
        out_shape=jax.ShapeDtypeStruct((n_seq * r, dh), F32),
        compiler_params=_params("parallel", "arbitrary"),
        name="moba_sample",
    )(page_table, q_ht, k_new.reshape(n_seq * r, dh), v_new.reshape(n_seq * r, dh),
      *([ck] * (2 * bps)), *([cv] * (2 * bps)))
    return out.reshape(n_seq, n_heads, t, dh).transpose(0, 2, 1, 3).reshape(n_seq * t, n_heads * dh)


def _mlstm_proj_kernel(x_ref, g_ref, w_ref, wgh_ref, wgl_ref, bg_ref, q_ref, k_ref, vt_ref, o_ref, gt_ref,
                       *, n_heads, hk, hv):
    xn = _rms(x_ref[...], g_ref[...])
    x_hi = xn.astype(BF16)
    z = _dot(x_hi, w_ref[...])
    dk = hk // n_heads
    q_ref[...] = z[:, :hk].astype(BF16)
    k_ref[...] = (z[:, hk:2 * hk] * dk ** -0.5).astype(BF16)
    vt_ref[...] = z[:, 2 * hk:2 * hk + hv].T.astype(BF16)
    o_ref[...] = jax.nn.sigmoid(z[:, 2 * hk + hv:])
    x_lo = (xn - x_hi.astype(F32)).astype(BF16)
    g = _dot(x_hi, wgh_ref[...]) + (_dot(x_hi, wgl_ref[...]) + _dot(x_lo, wgh_ref[...])) + bg_ref[...]
    c = lax.broadcasted_iota(jnp.int32, g.shape, 1)
    g = jnp.where((c >= n_heads) & (c < 2 * n_heads), _log_sigmoid(g), g)
    gt_ref[...] = g.T


def _mlstm_proj(x, gain, w_in, b_gates, n_heads, tm):
    t, dm = x.shape
    ng = 2 * n_heads
    hv = dm
    hk = (w_in.shape[1] - ng - 2 * hv) // 2
    w_main = w_in[:, :2 * hk + 2 * hv].astype(BF16)
    w_g = jnp.pad(w_in[:, 2 * hk + 2 * hv:], ((0, 0), (0, LANES - ng)))
    w_g_hi = w_g.astype(BF16)
    w_g_lo = (w_g - w_g_hi.astype(F32)).astype(BF16)
    b_g = jnp.pad(b_gates, (0, LANES - ng))
    row = lambda i: (i, 0)
    return pl.pallas_call(
        functools.partial(_mlstm_proj_kernel, n_heads=n_heads, hk=hk, hv=hv),
        grid=(t // tm,),
        in_specs=[pl.BlockSpec((tm, dm), row), _resident((1, dm)), _resident((dm, 2 * hk + 2 * hv)),
                  _resident((dm, LANES)), _resident((dm, LANES)), _resident((1, LANES))],
        out_specs=[pl.BlockSpec((tm, hk), row), pl.BlockSpec((tm, hk), row), pl.BlockSpec((hv, tm), lambda i: (0, i)),
                   pl.BlockSpec((tm, hv), row), pl.BlockSpec((LANES, tm), lambda i: (0, i))],
        out_shape=[jax.ShapeDtypeStruct((t, hk), BF16), jax.ShapeDtypeStruct((t, hk), BF16),
                   jax.ShapeDtypeStruct((hv, t), BF16), jax.ShapeDtypeStruct((t, hv), F32),
                   jax.ShapeDtypeStruct((LANES, t), F32)],
        compiler_params=_params("parallel"),
        name="mlstm_proj",
    )(x, gain.reshape(1, dm), w_main, w_g_hi, w_g_lo, b_g.reshape(1, LANES))


def _scan_lanes(x, op, fill):
    lane = lax.broadcasted_iota(jnp.int32, x.shape, 1)
    d = 1
    while d < x.shape[1]:
        x = op(x, jnp.where(lane >= d, pltpu.roll(x, d, 1), fill))
        d *= 2
    return x


def _mlstm_kernel(q_ref, k_ref, vt_ref, o_ref, gt_ref, hg_ref, c0_ref, n0_ref, m0_ref,
                  y_ref, c_ref, n_ref, m_ref, *, n_heads):
    @pl.when(pl.program_id(1) == 0)
    def _():
        c_ref[...] = c0_ref[...]
        n_ref[...] = n0_ref[...]
        m_ref[...] = m0_ref[...]

    ln = q_ref.shape[0]
    dk = q_ref.shape[1] // n_heads
    dv = vt_ref.shape[0] // n_heads
    gt = gt_ref[...]
    ig, fg = gt[:n_heads], gt[n_heads:2 * n_heads]
    bcum = _scan_lanes(fg, jnp.add, 0.0)
    a = ig - bcum
    m_prev = m_ref[0][:, :1]
    mx = jnp.maximum(m_prev, _scan_lanes(a, jnp.maximum, -jnp.inf))
    mx_last = mx[:, ln - 1:]
    inter = jnp.exp(m_prev - mx)
    floor = jnp.exp(-(bcum + mx))
    decay = jnp.exp(a - mx_last)
    carry = jnp.exp(m_prev - mx_last)
    a_cols = jnp.concatenate([a, jnp.zeros((LANES - n_heads, ln), F32)], axis=0).T
    si = lax.broadcasted_iota(jnp.int32, (ln, ln), 0)
    ti = lax.broadcasted_iota(jnp.int32, (ln, ln), 1)
    visible = si <= ti
    ones = jnp.ones((BF16_ROWS, ln), BF16)
    for h in range(n_heads):
        row = slice(h, h + 1)
        qh = q_ref[:, h * dk:(h + 1) * dk]
        kh = k_ref[:, h * dk:(h + 1) * dk]
        vth = vt_ref[h * dv:(h + 1) * dv, :]
        c_prev = c_ref[0, h]
        n_prev = n_ref[0, row, :]
        dmat = jnp.exp(jnp.where(visible, a_cols[:, row] - mx[row], -jnp.inf))
        w = dmat * _dot_nt(kh, qh)
        c_aug = jnp.concatenate([c_prev, jnp.broadcast_to(n_prev, (BF16_ROWS, dk))], axis=0).astype(BF16)
        v_aug = jnp.concatenate([vth, ones], axis=0)
        x = inter[row] * _dot_nt(c_aug, qh) + _dot(v_aug, w.astype(BF16))
        hh = x[:dv] / jnp.maximum(jnp.abs(x[dv:dv + 1]), floor[row])
        hn = hh * lax.rsqrt(jnp.mean(hh * hh, axis=0, keepdims=True) + RMS_EPS)
        y = hn.T * hg_ref[:, h * dv:(h + 1) * dv] * o_ref[:, h * dv:(h + 1) * dv]
        y_ref[:, h * dv:(h + 1) * dv] = y.astype(y_ref.dtype)
        vd = jnp.concatenate([vth.astype(F32) * decay[row], jnp.broadcast_to(decay[row], (BF16_ROWS, ln))], axis=0)
        upd = _dot(vd.astype(BF16), kh)
        c_ref[0, h] = carry[row] * c_prev + upd[:dv]
        n_ref[0, row, :] = carry[row] * n_prev + upd[dv:dv + 1]
    m_ref[0] = jnp.broadcast_to(bcum[:, ln - 1:] + mx_last, (n_heads, LANES))


def _mlstm_scan(q, k, vt, o, gates_t, h_gain, c0, n0, m0, n_seq, chunk, n_heads):
    t, hk = q.shape
    hv = vt.shape[0]
    nc = t // n_seq // chunk
    row = lambda s, c: (s * nc + c, 0)
    col = lambda s, c: (0, s * nc + c)
    st4 = pl.BlockSpec((1,) + c0.shape[1:], lambda s, c: (s, 0, 0, 0))
    st3n = pl.BlockSpec((1,) + n0.shape[1:], lambda s, c: (s, 0, 0))
    st3m = pl.BlockSpec((1,) + m0.shape[1:], lambda s, c: (s, 0, 0))
    return pl.pallas_call(
        functools.partial(_mlstm_kernel, n_heads=n_heads),
        grid=(n_seq, nc),
        in_specs=[pl.BlockSpec((chunk, hk), row), pl.BlockSpec((chunk, hk), row), pl.BlockSpec((hv, chunk), col),
                  pl.BlockSpec((chunk, hv), row), pl.BlockSpec((LANES, chunk), col),
                  pl.BlockSpec((1, hv), lambda s, c: (0, 0)), st4, st3n, st3m],
        out_specs=[pl.BlockSpec((chunk, hv), row), st4, st3n, st3m],
        out_shape=[jax.ShapeDtypeStruct((t, hv), BF16), jax.ShapeDtypeStruct(c0.shape, F32),
                   jax.ShapeDtypeStruct(n0.shape, F32), jax.ShapeDtypeStruct(m0.shape, F32)],
        compiler_params=_params("parallel", "arbitrary"),
        name="mlstm_scan",
    )(q, k, vt, o, gates_t, h_gain.reshape(1, hv), c0, n0, m0)


def _pad_rows(a, n_seq, chunk):
    t = a.shape[0] // n_seq
    a = a.reshape(n_seq, t, a.shape[1])
    a = jnp.pad(a, ((0, 0), (0, chunk - t), (0, 0)))
    return a.reshape(n_seq * chunk, a.shape[2])


def _mlp_kernel(x_ref, a_ref, wo_ref, g_ref, wu_ref, wd_ref, out_ref, *, ff_chunk):
    x1 = x_ref[...] + _dot(a_ref[...].astype(BF16), wo_ref[...])
    xn = _rms(x1, g_ref[...]).astype(BF16)
    acc = x1
    for c in range(wu_ref.shape[1] // ff_chunk):
        hid = _dot(xn, wu_ref[:, c * ff_chunk:(c + 1) * ff_chunk])
        hid = jnp.square(jnp.maximum(hid, 0.0)).astype(BF16)
        acc = acc + _dot(hid, wd_ref[c * ff_chunk:(c + 1) * ff_chunk, :])
    out_ref[...] = acc


def _mix_out_mlp(x, a, w_out_bf, gain, w_up_bf, w_down_bf, tm):
    t, dm = x.shape
    dff = w_up_bf.shape[1]
    row = lambda i: (i, 0)
    return pl.pallas_call(
        functools.partial(_mlp_kernel, ff_chunk=1024),
        grid=(t // tm,),
        in_specs=[pl.BlockSpec((tm, dm), row), pl.BlockSpec((tm, a.shape[1]), row), _resident(w_out_bf.shape),
                  _resident((1, dm)), _resident((dm, dff)), _resident((dff, dm))],
        out_specs=pl.BlockSpec((tm, dm), row),
        out_shape=jax.ShapeDtypeStruct((t, dm), F32),
        compiler_params=_params("parallel"),
        name="mix_out_mlp",
    )(x, a, w_out_bf, gain.reshape(1, dm), w_up_bf, w_down_bf)


def kernel(x_prompt, x_sample, cache_k, cache_v, state_C, state_n, state_m, page_table, norm_mix, norm_mlp,
           moba_w_qkv, moba_q_gain, moba_k_gain, moba_w_out, mlstm_w_in, mlstm_b_gates, mlstm_h_gain,
           mlstm_w_out, mlp_w_up, mlp_w_down):
    batch, seq, dm = x_prompt.shape
    n_dec, t_dec, _ = x_sample.shape
    depth = norm_mix.shape[0]
    att_heads, att_dh = cache_k.shape[3], cache_k.shape[4]
    ml_heads, ml_dv, ml_dk = state_C.shape[2], state_C.shape[3], state_C.shape[4]
    n_prompt_rows, n_sample_rows = batch * seq, n_dec * t_dec
    tm_p = 512
    nb = seq // MOBA_BLOCK
    nbp = -(-nb // SUBLANES) * SUBLANES
    assert seq % max(MOBA_BLOCK, PROMPT_CHUNK, tm_p) == 0 and t_dec <= SAMPLE_CHUNK

    xp = x_prompt.reshape(n_prompt_rows, dm)
    xs = x_sample.reshape(n_sample_rows, dm)
    outs = {name: [] for name in ("kp", "vp", "cp", "np", "mp", "ks", "vs", "cs", "ns", "ms")}

    for i in range(depth):
        l = i // 2
        if i % 2 == 0:
            w_qkv = moba_w_qkv[l].astype(BF16)
            qp, kp, vp, kp_bf, vtp_bf, kmean = _moba_proj(xp, norm_mix[i], w_qkv, moba_q_gain[l], moba_k_gain[l],
                                                          att_heads, tm_p, True)
            kmean = jnp.pad(kmean.reshape(batch, nb, -1), ((0, 0), (0, nbp - nb), (0, 0)))
            bound = (1.01 * att_dh ** 0.5 * math.log2(math.e)
                     * jnp.max(jnp.abs(moba_q_gain[l])) * jnp.max(jnp.abs(moba_k_gain[l]))).reshape(1, 1)
            ap = _moba_attn(qp, kp_bf, vtp_bf, kmean.reshape(batch * nbp, -1), bound, batch, seq, att_heads)
            qs, ks, vs = _moba_proj(xs, norm_mix[i], w_qkv, moba_q_gain[l], moba_k_gain[l],
                                    att_heads, n_sample_rows, False)
            a_s = _moba_sample(qs, ks, vs, cache_k, cache_v, page_table, l, n_dec, att_heads)
            outs["kp"].append(kp); outs["vp"].append(vp); outs["ks"].append(ks); outs["vs"].append(vs)
            w_out = moba_w_out[l].astype(BF16)
        else:
            zc = jnp.zeros((batch, ml_heads, ml_dv, ml_dk), F32)
            zn = jnp.zeros((batch, ml_heads, ml_dk), F32)
            zm = jnp.zeros((batch, ml_heads, LANES), F32)
            q, k, vt, o, gt = _mlstm_proj(xp, norm_mix[i], mlstm_w_in[l], mlstm_b_gates[l], ml_heads, tm_p)
            ap, c_p, n_p, m_p = _mlstm_scan(q, k, vt, o, gt, mlstm_h_gain[l], zc, zn, zm,
                                            batch, PROMPT_CHUNK, ml_heads)
            q, k, vt, o, gt = _mlstm_proj(xs, norm_mix[i], mlstm_w_in[l], mlstm_b_gates[l], ml_heads, n_sample_rows)
            pad = lambda a: _pad_rows(a, n_dec, SAMPLE_CHUNK)
            pad_t = lambda a: _pad_rows(a.T, n_dec, SAMPLE_CHUNK).T
            idle = jnp.where(jnp.arange(LANES) < ml_heads, -jnp.inf, 0.0)
            real = (jnp.arange(n_dec * SAMPLE_CHUNK) % SAMPLE_CHUNK) < t_dec
            gt_pad = jnp.where(real[None, :], pad_t(gt), idle[:, None])
            m0 = jnp.broadcast_to(state_m[l][:, :, None], (n_dec, ml_heads, LANES))
            a_s, c_s, n_s, m_s = _mlstm_scan(pad(q), pad(k), pad_t(vt), pad(o), gt_pad, mlstm_h_gain[l],
                                             state_C[l], state_n[l], m0, n_dec, SAMPLE_CHUNK, ml_heads)
            a_s = a_s.reshape(n_dec, SAMPLE_CHUNK, -1)[:, :t_dec].reshape(n_sample_rows, -1)
            outs["cp"].append(c_p); outs["np"].append(n_p); outs["mp"].append(m_p[:, :, 0])
            outs["cs"].append(c_s); outs["ns"].append(n_s); outs["ms"].append(m_s[:, :, 0])
            w_out = mlstm_w_out[l].astype(BF16)
        w_up, w_down = mlp_w_up[i].astype(BF16), mlp_w_down[i].astype(BF16)
        xp = _mix_out_mlp(xp, ap, w_out, norm_mlp[i], w_up, w_down, tm_p)
        xs = _mix_out_mlp(xs, a_s, w_out, norm_mlp[i], w_up, w_down, n_sample_rows)

    kv_p = (len(outs["kp"]), batch, seq, att_heads, att_dh)
    kv_s = (len(outs["ks"]), n_dec, t_dec, att_heads, att_dh)
    return (xp.reshape(batch, seq, dm), xs.reshape(n_dec, t_dec, dm),
            jnp.stack(outs["kp"]).reshape(kv_p), jnp.stack(outs["vp"]).reshape(kv_p),
            jnp.stack(outs["cp"]), jnp.stack(outs["np"]), jnp.stack(outs["mp"]),
            jnp.stack(outs["ks"]).reshape(kv_s), jnp.stack(outs["vs"]).reshape(kv_s),
            jnp.stack(outs["cs"]), jnp.stack(outs["ns"]), jnp.stack(outs["ms"]))
```

```python
import functools
import math

import jax
import jax.numpy as jnp
from jax import lax
from jax.experimental import pallas as pl
from jax.experimental.pallas import tpu as pltpu

F32 = jnp.float32
BF16 = jnp.bfloat16
HIGHEST = lax.Precision.HIGHEST

RMS_EPS = 1e-6
MOBA_BLOCK = 256
MOBA_TOPK = 3
LANES = 128
SUBLANES = 8
BF16_ROWS = 16
SAMPLE_CHUNK = 128
PROMPT_CHUNK = 256
ATTN_HEADS_PER_STEP = 4
SAMPLE_BLOCKS_PER_STEP = 8
NEG = -1e30
FIXED_SHIFT_MAX = 60.0
VMEM_LIMIT = 56 * 1024 * 1024


def _params(*sem):
    return pltpu.CompilerParams(dimension_semantics=sem, vmem_limit_bytes=VMEM_LIMIT)


def _dot(a, b, precision=None):
    return jnp.dot(a, b, preferred_element_type=F32, precision=precision)


def _dot_nt(a, b, precision=None):
    return lax.dot_general(a, b, (((1,), (1,)), ((), ())), preferred_element_type=F32, precision=precision)


def _dot3_nt(a, b):
    a_hi, b_hi = a.astype(BF16), b.astype(BF16)
    a_lo, b_lo = (a - a_hi.astype(F32)).astype(BF16), (b - b_hi.astype(F32)).astype(BF16)
    return _dot_nt(a_hi, b_hi) + (_dot_nt(a_hi, b_lo) + _dot_nt(a_lo, b_hi))


def _dot_tn(a, b):
    return lax.dot_general(a, b, (((0,), (0,)), ((), ())), preferred_element_type=F32)


def _rms(x, g):
    return x * lax.rsqrt(jnp.mean(x * x, axis=-1, keepdims=True) + RMS_EPS) * g


def _log_sigmoid(x):
    return -(jnp.maximum(-x, 0.0) + jnp.log1p(jnp.exp(-jnp.abs(x))))


def _resident(shape):
    return pl.BlockSpec(shape, lambda *_: (0,) * len(shape), pipeline_mode=pl.Buffered(1))


def _top_blocks(g, axis):
    idx = lax.broadcasted_iota(jnp.int32, g.shape, axis).astype(F32)
    sel = jnp.zeros(g.shape, F32)
    for _ in range(MOBA_TOPK):
        m = jnp.max(g, axis=axis, keepdims=True)
        first = jnp.min(jnp.where(g == m, idx, 1e9), axis=axis, keepdims=True)
        pick = idx == jnp.where(m > -jnp.inf, first, -1.0)
        sel = jnp.where(pick, 1.0, sel)
        g = jnp.where(pick, -jnp.inf, g)
    return sel


def _moba_proj_kernel(x_ref, g_ref, w_ref, qg_ref, kg_ref, q_ref, k_ref, v_ref,
                      kb_ref=None, vt_ref=None, km_ref=None, *, n_heads, dh):
    d = n_heads * dh
    xn = _rms(x_ref[...], g_ref[...])
    qkv = _dot(xn.astype(BF16), w_ref[...])
    for h in range(n_heads):
        sl = slice(h * dh, (h + 1) * dh)
        q_ref[:, sl] = _rms(qkv[:, h * dh:(h + 1) * dh], qg_ref[...])
        kn = _rms(qkv[:, d + h * dh:d + (h + 1) * dh], kg_ref[...])
        k_ref[:, sl] = kn
        if kb_ref is not None:
            kb_ref[:, sl] = kn.astype(BF16)
    v = qkv[:, 2 * d:]
    v_ref[...] = v
    if vt_ref is not None:
        vt_ref[...] = v.T.astype(BF16)
    if km_ref is not None:
        for r in range(km_ref.shape[0]):
            km_ref[r] = jnp.mean(k_ref[r * MOBA_BLOCK:(r + 1) * MOBA_BLOCK, :], axis=0, keepdims=True)


def _moba_proj(x, gain, w_bf, q_gain, k_gain, n_heads, tm, for_prompt):
    t, dm = x.shape
    d = w_bf.shape[1] // 3
    dh = d // n_heads
    row = lambda i: (i, 0)
    out_specs = [pl.BlockSpec((tm, d), row)] * 3
    out_shape = [jax.ShapeDtypeStruct((t, d), F32)] * 3
    if for_prompt:
        out_specs += [pl.BlockSpec((tm, d), row), pl.BlockSpec((d, tm), lambda i: (0, i)),
                      pl.BlockSpec((tm // MOBA_BLOCK, 1, d), lambda i: (i, 0, 0))]
        out_shape += [jax.ShapeDtypeStruct((t, d), BF16), jax.ShapeDtypeStruct((d, t), BF16),
                      jax.ShapeDtypeStruct((t // MOBA_BLOCK, 1, d), F32)]
    return pl.pallas_call(
        functools.partial(_moba_proj_kernel, n_heads=n_heads, dh=dh),
        grid=(t // tm,),
        in_specs=[pl.BlockSpec((tm, dm), row), _resident((1, dm)), _resident((dm, 3 * d)),
                  _resident((1, dh)), _resident((1, dh))],
        out_specs=out_specs,
        out_shape=out_shape,
        compiler_params=_params("parallel"),
        name="moba_proj",
    )(x, gain.reshape(1, dm), w_bf, q_gain.reshape(1, dh), k_gain.reshape(1, dh))


def _moba_attn_kernel(bound_ref, q_ref, k_ref, vt_ref, km_ref, o_ref, bias_ref, s_ref, *, dh, scale):
    blk = q_ref.shape[0]
    step = 2 * blk
    i = pl.program_id(2)
    heads = range(q_ref.shape[1] // dh)
    nbp = km_ref.shape[0]
    blk_i = lax.broadcasted_iota(jnp.int32, (nbp, blk), 0)
    causal =(lax.broadcasted_iota(jnp.int32, (blk, blk), 0) <= lax.broadcasted_iota(jnp.int32, (blk, blk), 1))
    start = pl.multiple_of(i * blk, blk)
    last_trip = k_ref.shape[0] // step - 1
    trips = (i + 1) // 2
    bound = bound_ref[0, 0]

    def scores(t, h, qb):
        st = pl.multiple_of(jnp.minimum(t, last_trip) * step, step)
        return _dot_nt(k_ref[pl.ds(st, step), h * dh:(h + 1) * dh], qb)

    def bias_rows(t, h):
        return jnp.concatenate(
            [jnp.broadcast_to(bias_ref[h, pl.ds(2 * t + u, 1), :], (blk, blk)) for u in range(2)], axis=0)

    qbs = []
    for h in heads:
        sl = slice(h * dh, (h + 1) * dh)
        q = q_ref[:, sl]
        gates = _dot3_nt(km_ref[:, sl], q)
        sel = _top_blocks(jnp.where(blk_i < i, gates, -jnp.inf), 0)
        bias_ref[h, :nbp, :] = jnp.where(sel > 0, 0.0, NEG)
        bias_ref[h, nbp:, :] = jnp.full((bias_ref.shape[1] - nbp, blk), NEG, F32)
        qbs.append((q * scale).astype(BF16))

    def own_block(h):
        sl = slice(h * dh, (h + 1) * dh)
        s = _dot_nt(k_ref[pl.ds(start, blk), sl], qbs[h])
        return jnp.where(causal, s, -jnp.inf), vt_ref[sl, pl.ds(start, blk)]

    def fixed_shift():
        state = []
        for h in heads:
            s, vt = own_block(h)
            p = jnp.exp2(s - bound)
            state.append((jnp.sum(p, axis=0, keepdims=True), _dot(vt, p.astype(BF16))))
            for u in range(2):
                s_ref[u, h] = scores(u, h, qbs[h])

        def body(tt, carry):
            out = list(carry)
            for u in range(2):
                t = 2 * tt + u
                st = pl.multiple_of(jnp.minimum(t, last_trip) * step, step)
                for h in heads:
                    l, acc = out[h]
                    p = jnp.exp2(s_ref[u, h] + (bias_rows(t, h) - bound))
                    s_ref[u, h] = scores(t + 2, h, qbs[h])
                    l = l + jnp.sum(p, axis=0, keepdims=True)
                    acc = acc + _dot(vt_ref[h * dh:(h + 1) * dh, pl.ds(st, step)], p.astype(BF16))
                    out[h] = (l, acc)
            return tuple(out)

        return lax.fori_loop(0, (trips + 1) // 2, body, tuple(state))

    def running_max():
        state = []
        for h in heads:
            s, vt = own_block(h)
            m = jnp.max(s, axis=0, keepdims=True)
            p = jnp.exp2(s - m)
            state.append((m, jnp.sum(p, axis=0, keepdims=True), _dot(vt, p.astype(BF16)), scores(0, h, qbs[h])))

        def body(t, carry):
            st = pl.multiple_of(t * step, step)
            out = []
            for h in heads:
                m, l, acc, s = carry[h]
                s_next = scores(t + 1, h, qbs[h])
                s = s + bias_rows(t, h)
                mn = jnp.maximum(m, jnp.max(s, axis=0, keepdims=True))
                alpha = jnp.exp2(m - mn)
                p = jnp.exp2(s - mn)
                l = alpha * l + jnp.sum(p, axis=0, keepdims=True)
                acc = alpha * acc + _dot(vt_ref[h * dh:(h + 1) * dh, pl.ds(st, step)], p.astype(BF16))
                out.append((mn, l, acc, s_next))
            return tuple(out)

        return tuple(c[1:3] for c in lax.fori_loop(0, trips, body, tuple(state)))

    final = lax.cond(bound <= FIXED_SHIFT_MAX, fixed_shift, running_max)
    for h in heads:
        l, acc = final[h]
        o_ref[:, h * dh:(h + 1) * dh] = (acc / l).T.astype(o_ref.dtype)


def _moba_attn(q, k_bf, vt_bf, k_mean, score_bound, batch, seq, n_heads):
    d = q.shape[1]
    dh = d // n_heads
    nb = seq // MOBA_BLOCK
    nbp = k_mean.shape[0] // batch
    hb = ATTN_HEADS_PER_STEP
    assert nb % 2 == 0 and n_heads % hb == 0
    return pl.pallas_call(
        functools.partial(_moba_attn_kernel, dh=dh, scale=dh ** -0.5 * math.log2(math.e)),
        grid=(batch, n_heads // hb, nb),
        in_specs=[pl.BlockSpec(memory_space=pltpu.SMEM),
                  pl.BlockSpec((MOBA_BLOCK, hb * dh), lambda b, h, i: (b * nb + i, h)),
                  pl.BlockSpec((seq, hb * dh), lambda b, h, i: (b, h)),
                  pl.BlockSpec((hb * dh, seq), lambda b, h, i: (h, b)),
                  pl.BlockSpec((nbp, hb * dh), lambda b, h, i: (b, h))],
        out_specs=pl.BlockSpec((MOBA_BLOCK, hb * dh), lambda b, h, i: (b * nb + i, h)),
        out_shape=jax.ShapeDtypeStruct((batch * seq, d), BF16),
        scratch_shapes=[pltpu.VMEM((hb, nbp + SUBLANES, MOBA_BLOCK), F32),
                        pltpu.VMEM((2, hb, 2 * MOBA_BLOCK, MOBA_BLOCK), F32)],
        compiler_params=_params("parallel", "parallel", "arbitrary"),
        name="moba_attn",
    )(score_bound, q, k_bf, vt_bf, k_mean)


def _sum_row_groups(x, rows):
    while x.shape[0] > rows:
        half = x.shape[0] // 2
        x = x[:half] + x[half:]
    return x


def _moba_sample_kernel(pt_ref, q_ref, kn_ref, vn_ref, *refs, n_heads, scale):
    del pt_ref
    bps = SAMPLE_BLOCKS_PER_STEP
    k_refs, v_refs = refs[:2 * bps], refs[2 * bps:4 * bps]
    o_ref, mask_ref, g_all, m_all, l_all, o_all = refs[4 * bps:]
    j = pl.program_id(1)
    r, dh = q_ref.shape
    t = r // n_heads
    rows = k_refs[0].shape[0]

    @pl.when(j == 0)
    def _():
        row_head = lax.broadcasted_iota(jnp.int32, (r, rows), 0) // t
        col_head = lax.rem(lax.broadcasted_iota(jnp.int32, (r, rows), 1), n_heads)
        mask_ref[...] = jnp.where(row_head == col_head, 0.0, -jnp.inf)

    q = q_ref[...]
    qb = (q * scale).astype(BF16)
    mask = mask_ref[...]
    for u in range(bps):
        k0 = k_refs[2 * u][...]
        k1 = k_refs[2 * u + 1][...]
        s0 = _dot_nt(qb, k0.astype(BF16)) + mask
        s1 = _dot_nt(qb, k1.astype(BF16)) + mask
        m = jnp.maximum(jnp.max(s0, axis=1, keepdims=True), jnp.max(s1, axis=1, keepdims=True))
        p0 = jnp.exp2(s0 - m)
        p1 = jnp.exp2(s1 - m)
        l = jnp.sum(p0, axis=1, keepdims=True) + jnp.sum(p1, axis=1, keepdims=True)
        o = (_dot(p0.astype(BF16), v_refs[2 * u][...].astype(BF16))
             + _dot(p1.astype(BF16), v_refs[2 * u + 1][...].astype(BF16)))
        k_mean = (_sum_row_groups(k0, n_heads) + _sum_row_groups(k1, n_heads)) * (n_heads / (2.0 * rows))
        k_mean = jnp.concatenate([jnp.broadcast_to(k_mean[h:h + 1, :], (t, dh)) for h in range(n_heads)], axis=0)
        gate = jnp.sum(q * k_mean, axis=1, keepdims=True)
        g_all[bps * j + u] = jnp.broadcast_to(gate, (r, LANES))
        m_all[bps * j + u] = jnp.broadcast_to(m, (r, LANES))
        l_all[bps * j + u] = jnp.broadcast_to(l, (r, LANES))
        o_all[bps * j + u] = o

    @pl.when(j == pl.num_programs(1) - 1)
    def _():
        pad = jnp.zeros((LANES - r, dh), F32)
        kp = jnp.concatenate([kn_ref[...], pad], axis=0).astype(BF16)
        vp = jnp.concatenate([vn_ref[...], pad], axis=0).astype(BF16)
        so = _dot_nt(qb, kp)
        row = lax.broadcasted_iota(jnp.int32, so.shape, 0)
        col = lax.broadcasted_iota(jnp.int32, so.shape, 1)
        same_head = (row // t) == lax.rem(col, n_heads)
        causal = (col // n_heads) <= lax.rem(row, t)
        so = jnp.where(same_head, jnp.where(causal, so, -jnp.inf), -jnp.inf)
        m_own = jnp.max(so, axis=1, keepdims=True)
        po = jnp.exp2(so - m_own)
        l_own = jnp.sum(po, axis=1, keepdims=True)
        o_own = _dot(po.astype(BF16), vp)

        sel = _top_blocks(g_all[...], 0)
        ma = m_all[...]
        mm = jnp.maximum(jnp.max(jnp.where(sel > 0, ma, -jnp.inf), axis=0), m_own)
        w = jnp.where(sel > 0, jnp.exp2(ma - mm), 0.0)
        w_own = jnp.exp2(m_own - mm)
        den = jnp.sum(w * l_all[...], axis=0) + w_own * l_own
        num = jnp.sum(w * o_all[...], axis=0) + w_own * o_own
        o_ref[...] = num / den


def _moba_sample(q, k_new, v_new, cache_k, cache_v, page_table, layer, n_seq, n_heads):
    n_layers, pool, page, _, dh = cache_k.shape
    t = q.shape[0] // n_seq
    r = n_heads * t
    rows = page * n_heads
    bps = SAMPLE_BLOCKS_PER_STEP
    assert MOBA_BLOCK == 2 * page and r <= LANES and dh == LANES
    nblk = page_table.shape[1] // 2
    assert nblk >= MOBA_TOPK and nblk % bps == 0
    ck = cache_k.reshape(n_layers, pool, rows, dh)
    cv = cache_v.reshape(n_layers, pool, rows, dh)
    q_ht = q.reshape(n_seq, t, n_heads, dh).transpose(0, 2, 1, 3).reshape(n_seq * r, dh)
    tok = pl.BlockSpec((r, dh), lambda s, j, pt: (s, 0))
    pg = lambda o: pl.BlockSpec((None, None, rows, dh), lambda s, j, pt: (layer, pt[s, 2 * bps * j + o], 0, 0))
    pages = [pg(o) for o in range(2 * bps)]
    stat = pltpu.VMEM((nblk, r, LANES), F32)
    out = pl.pallas_call(
        functools.partial(_moba_sample_kernel, n_heads=n_heads, scale=dh ** -0.5 * math.log2(math.e)),
        grid_spec=pltpu.PrefetchScalarGridSpec(
            num_scalar_prefetch=1,
            grid=(n_seq, nblk // bps),
            in_specs=[tok, tok, tok] + pages + pages,
            out_specs=tok,
            scratch_shapes=[pltpu.VMEM((r, rows), F32), stat, stat, stat, stat]),
        out_shape=jax.ShapeDtypeStruct((n_seq * r, dh), F32),
        compiler_params=_params("parallel", "arbitrary"),
        name="moba_sample",
    )(page_table, q_ht, k_new.reshape(n_seq * r, dh), v_new.reshape(n_seq * r, dh),
      *([ck] * (2 * bps)), *([cv] * (2 * bps)))
    return out.reshape(n_seq, n_heads, t, dh).transpose(0, 2, 1, 3).reshape(n_seq * t, n_heads * dh)


def _mlstm_proj_kernel(x_ref, g_ref, w_ref, wgh_ref, wgl_ref, bg_ref, q_ref, k_ref, vt_ref, o_ref, gt_ref,
                       *, n_heads, hk, hv):
    xn = _rms(x_ref[...], g_ref[...])
    x_hi = xn.astype(BF16)
    z = _dot(x_hi, w_ref[...])
    dk = hk // n_heads
    q_ref[...] = z[:, :hk].astype(BF16)
    k_ref[...] = (z[:, hk:2 * hk] * dk ** -0.5).astype(BF16)
    vt_ref[...] = z[:, 2 * hk:2 * hk + hv].T.astype(BF16)
    o_ref[...] = jax.nn.sigmoid(z[:, 2 * hk + hv:])
    x_lo = (xn - x_hi.astype(F32)).astype(BF16)
    g = _dot(x_hi, wgh_ref[...]) + (_dot(x_hi, wgl_ref[...]) + _dot(x_lo, wgh_ref[...])) + bg_ref[...]
    c = lax.broadcasted_iota(jnp.int32, g.shape, 1)
    g = jnp.where((c >= n_heads) & (c < 2 * n_heads), _log_sigmoid(g), g)
    gt_ref[...] = g.T


def _mlstm_proj(x, gain, w_in, b_gates, n_heads, tm):
    t, dm = x.shape
    ng = 2 * n_heads
    hv = dm
    hk = (w_in.shape[1] - ng - 2 * hv) // 2
    w_main = w_in[:, :2 * hk + 2 * hv].astype(BF16)
    w_g = jnp.pad(w_in[:, 2 * hk + 2 * hv:], ((0, 0), (0, LANES - ng)))
    w_g_hi = w_g.astype(BF16)
    w_g_lo = (w_g - w_g_hi.astype(F32)).astype(BF16)
    b_g = jnp.pad(b_gates, (0, LANES - ng))
    row = lambda i: (i, 0)
    return pl.pallas_call(
        functools.partial(_mlstm_proj_kernel, n_heads=n_heads, hk=hk, hv=hv),
        grid=(t // tm,),
        in_specs=[pl.BlockSpec((tm, dm), row), _resident((1, dm)), _resident((dm, 2 * hk + 2 * hv)),
                  _resident((dm, LANES)), _resident((dm, LANES)), _resident((1, LANES))],
        out_specs=[pl.BlockSpec((tm, hk), row), pl.BlockSpec((tm, hk), row), pl.BlockSpec((hv, tm), lambda i: (0, i)),
                   pl.BlockSpec((tm, hv), row), pl.BlockSpec((LANES, tm), lambda i: (0, i))],
        out_shape=[jax.ShapeDtypeStruct((t, hk), BF16), jax.ShapeDtypeStruct((t, hk), BF16),
                   jax.ShapeDtypeStruct((hv, t), BF16), jax.ShapeDtypeStruct((t, hv), F32),
                   jax.ShapeDtypeStruct((LANES, t), F32)],
        compiler_params=_params("parallel"),
        name="mlstm_proj",
    )(x, gain.reshape(1, dm), w_main, w_g_hi, w_g_lo, b_g.reshape(1, LANES))


def _scan_lanes(x, op, fill):
    lane = lax.broadcasted_iota(jnp.int32, x.shape, 1)
    d = 1
    while d < x.shape[1]:
        x = op(x, jnp.where(lane >= d, pltpu.roll(x, d, 1), fill))
        d *= 2
    return x


def _mlstm_kernel(q_ref, k_ref, vt_ref, o_ref, gt_ref, hg_ref, c0_ref, n0_ref, m0_ref,
                  y_ref, c_ref, n_ref, m_ref, *, n_heads):
    @pl.when(pl.program_id(1) == 0)
    def _():
        c_ref[...] = c0_ref[...]
        n_ref[...] = n0_ref[...]
        m_ref[...] = m0_ref[...]

    ln = q_ref.shape[0]
    dk = q_ref.shape[1] // n_heads
    dv = vt_ref.shape[0] // n_heads
    gt = gt_ref[...]
    ig, fg = gt[:n_heads], gt[n_heads:2 * n_heads]
    bcum = _scan_lanes(fg, jnp.add, 0.0)
    a = ig - bcum
    m_prev = m_ref[0][:, :1]
    mx = jnp.maximum(m_prev, _scan_lanes(a, jnp.maximum, -jnp.inf))
    mx_last = mx[:, ln - 1:]
    inter = jnp.exp(m_prev - mx)
    floor = jnp.exp(-(bcum + mx))
    decay = jnp.exp(a - mx_last)
    carry = jnp.exp(m_prev - mx_last)
    a_cols = jnp.concatenate([a, jnp.zeros((LANES - n_heads, ln), F32)], axis=0).T
    si = lax.broadcasted_iota(jnp.int32, (ln, ln), 0)
    ti = lax.broadcasted_iota(jnp.int32, (ln, ln), 1)
    visible = si <= ti
    ones = jnp.ones((BF16_ROWS, ln), BF16)
    for h in range(n_heads):
        row = slice(h, h + 1)
        qh = q_ref[:, h * dk:(h + 1) * dk]
        kh = k_ref[:, h * dk:(h + 1) * dk]
        vth = vt_ref[h * dv:(h + 1) * dv, :]
        c_prev = c_ref[0, h]
        n_prev = n_ref[0, row, :]
        dmat = jnp.exp(jnp.where(visible, a_cols[:, row] - mx[row], -jnp.inf))
        w = dmat * _dot_nt(kh, qh)
        c_aug = jnp.concatenate([c_prev, jnp.broadcast_to(n_prev, (BF16_ROWS, dk))], axis=0).astype(BF16)
        v_aug = jnp.concatenate([vth, ones], axis=0)
        x = inter[row] * _dot_nt(c_aug, qh) + _dot(v_aug, w.astype(BF16))
        hh = x[:dv] / jnp.maximum(jnp.abs(x[dv:dv + 1]), floor[row])
        hn = hh * lax.rsqrt(jnp.mean(hh * hh, axis=0, keepdims=True) + RMS_EPS)
        y = hn.T * hg_ref[:, h * dv:(h + 1) * dv] * o_ref[:, h * dv:(h + 1) * dv]
        y_ref[:, h * dv:(h + 1) * dv] = y.astype(y_ref.dtype)
        vd = jnp.concatenate([vth.astype(F32) * decay[row], jnp.broadcast_to(decay[row], (BF16_ROWS, ln))], axis=0)
        upd = _dot(vd.astype(BF16), kh)
        c_ref[0, h] = carry[row] * c_prev + upd[:dv]
        n_ref[0, row, :] = carry[row] * n_prev + upd[dv:dv + 1]
    m_ref[0] = jnp.broadcast_to(bcum[:, ln - 1:] + mx_last, (n_heads, LANES))


def _mlstm_scan(q, k, vt, o, gates_t, h_gain, c0, n0, m0, n_seq, chunk, n_heads):
    t, hk = q.shape
    hv = vt.shape[0]
    nc = t // n_seq // chunk
    row = lambda s, c: (s * nc + c, 0)
    col = lambda s, c: (0, s * nc + c)
    st4 = pl.BlockSpec((1,) + c0.shape[1:], lambda s, c: (s, 0, 0, 0))
    st3n = pl.BlockSpec((1,) + n0.shape[1:], lambda s, c: (s, 0, 0))
    st3m = pl.BlockSpec((1,) + m0.shape[1:], lambda s, c: (s, 0, 0))
    return pl.pallas_call(
        functools.partial(_mlstm_kernel, n_heads=n_heads),
        grid=(n_seq, nc),
        in_specs=[pl.BlockSpec((chunk, hk), row), pl.BlockSpec((chunk, hk), row), pl.BlockSpec((hv, chunk), col),
                  pl.BlockSpec((chunk, hv), row), pl.BlockSpec((LANES, chunk), col),
                  pl.BlockSpec((1, hv), lambda s, c: (0, 0)), st4, st3n, st3m],
        out_specs=[pl.BlockSpec((chunk, hv), row), st4, st3n, st3m],
        out_shape=[jax.ShapeDtypeStruct((t, hv), BF16), jax.ShapeDtypeStruct(c0.shape, F32),
                   jax.ShapeDtypeStruct(n0.shape, F32), jax.ShapeDtypeStruct(m0.shape, F32)],
        compiler_params=_params("parallel", "arbitrary"),
        name="mlstm_scan",
    )(q, k, vt, o, gates_t, h_gain.reshape(1, hv), c0, n0, m0)


def _pad_rows(a, n_seq, chunk):
    t = a.shape[0] // n_seq
    a = a.reshape(n_seq, t, a.shape[1])
    a = jnp.pad(a, ((0, 0), (0, chunk - t), (0, 0)))
    return a.reshape(n_seq * chunk, a.shape[2])


def _mlp_kernel(x_ref, a_ref, wo_ref, g_ref, wu_ref, wd_ref, out_ref, *, ff_chunk):
    x1 = x_ref[...] + _dot(a_ref[...].astype(BF16), wo_ref[...])
    xn = _rms(x1, g_ref[...]).astype(BF16)
    acc = x1
    for c in range(wu_ref.shape[1] // ff_chunk):
        hid = _dot(xn, wu_ref[:, c * ff_chunk:(c + 1) * ff_chunk])
        hid = jnp.square(jnp.maximum(hid, 0.0)).astype(BF16)
        acc = acc + _dot(hid, wd_ref[c * ff_chunk:(c + 1) * ff_chunk, :])
    out_ref[...] = acc


def _mix_out_mlp(x, a, w_out_bf, gain, w_up_bf, w_down_bf, tm):
    t, dm = x.shape
    dff = w_up_bf.shape[1]
    row = lambda i: (i, 0)
    return pl.pallas_call(
        functools.partial(_mlp_kernel, ff_chunk=1024),
        grid=(t // tm,),
        in_specs=[pl.BlockSpec((tm, dm), row), pl.BlockSpec((tm, a.shape[1]), row), _resident(w_out_bf.shape),
                  _resident((1, dm)), _resident((dm, dff)), _resident((dff, dm))],
        out_specs=pl.BlockSpec((tm, dm), row),
        out_shape=jax.ShapeDtypeStruct((t, dm), F32),
        compiler_params=_params("parallel"),
        name="mix_out_mlp",
    )(x, a, w_out_bf, gain.reshape(1, dm), w_up_bf, w_down_bf)


def kernel(x_prompt, x_sample, cache_k, cache_v, state_C, state_n, state_m, page_table, norm_mix, norm_mlp,
           moba_w_qkv, moba_q_gain, moba_k_gain, moba_w_out, mlstm_w_in, mlstm_b_gates, mlstm_h_gain,
           mlstm_w_out, mlp_w_up, mlp_w_down):
    batch, seq, dm = x_prompt.shape
    n_dec, t_dec, _ = x_sample.shape
    depth = norm_mix.shape[0]
    att_heads, att_dh = cache_k.shape[3], cache_k.shape[4]
    ml_heads, ml_dv, ml_dk = state_C.shape[2], state_C.shape[3], state_C.shape[4]
    n_prompt_rows, n_sample_rows = batch * seq, n_dec * t_dec
    tm_p = 512
    nb = seq // MOBA_BLOCK
    nbp = -(-nb // SUBLANES) * SUBLANES
    assert seq % max(MOBA_BLOCK, PROMPT_CHUNK, tm_p) == 0 and t_dec <= SAMPLE_CHUNK

    xp = x_prompt.reshape(n_prompt_rows, dm)
    xs = x_sample.reshape(n_sample_rows, dm)
    outs = {name: [] for name in ("kp", "vp", "cp", "np", "mp", "ks", "vs", "cs", "ns", "ms")}

    for i in range(depth):
        l = i // 2
        if i % 2 == 0:
            w_qkv = moba_w_qkv[l].astype(BF16)
            qp, kp, vp, kp_bf, vtp_bf, kmean = _moba_proj(xp, norm_mix[i], w_qkv, moba_q_gain[l], moba_k_gain[l],
                                                          att_heads, tm_p, True)
            kmean = jnp.pad(kmean.reshape(batch, nb, -1), ((0, 0), (0, nbp - nb), (0, 0)))
            bound = (1.01 * att_dh ** 0.5 * math.log2(math.e)
                     * jnp.max(jnp.abs(moba_q_gain[l])) * jnp.max(jnp.abs(moba_k_gain[l]))).reshape(1, 1)
            ap = _moba_attn(qp, kp_bf, vtp_bf, kmean.reshape(batch * nbp, -1), bound, batch, seq, att_heads)
            qs, ks, vs = _moba_proj(xs, norm_mix[i], w_qkv, moba_q_gain[l], moba_k_gain[l],
                                    att_heads, n_sample_rows, False)
            a_s = _moba_sample(qs, ks, vs, cache_k, cache_v, page_table, l, n_dec, att_heads)
            outs["kp"].append(kp); outs["vp"].append(vp); outs["ks"].append(ks); outs["vs"].append(vs)
            w_out = moba_w_out[l].astype(BF16)
        else:
            zc = jnp.zeros((batch, ml_heads, ml_dv, ml_dk), F32)
            zn = jnp.zeros((batch, ml_heads, ml_dk), F32)
            zm = jnp.zeros((batch, ml_heads, LANES), F32)
            q, k, vt, o, gt = _mlstm_proj(xp, norm_mix[i], mlstm_w_in[l], mlstm_b_gates[l], ml_heads, tm_p)
            ap, c_p, n_p, m_p = _mlstm_scan(q, k, vt, o, gt, mlstm_h_gain[l], zc, zn, zm,
                                            batch, PROMPT_CHUNK, ml_heads)
            q, k, vt, o, gt = _mlstm_proj(xs, norm_mix[i], mlstm_w_in[l], mlstm_b_gates[l], ml_heads, n_sample_rows)
            pad = lambda a: _pad_rows(a, n_dec, SAMPLE_CHUNK)
            pad_t = lambda a: _pad_rows(a.T, n_dec, SAMPLE_CHUNK).T
            idle = jnp.where(jnp.arange(LANES) < ml_heads, -jnp.inf, 0.0)
            real = (jnp.arange(n_dec * SAMPLE_CHUNK) % SAMPLE_CHUNK) < t_dec
            gt_pad = jnp.where(real[None, :], pad_t(gt), idle[:, None])
            m0 = jnp.broadcast_to(state_m[l][:, :, None], (n_dec, ml_heads, LANES))
            a_s, c_s, n_s, m_s = _mlstm_scan(pad(q), pad(k), pad_t(vt), pad(o), gt_pad, mlstm_h_gain[l],
                                             state_C[l], state_n[l], m0, n_dec, SAMPLE_CHUNK, ml_heads)
            a_s = a_s.reshape(n_dec, SAMPLE_CHUNK, -1)[:, :t_dec].reshape(n_sample_rows, -1)
            outs["cp"].append(c_p); outs["np"].append(n_p); outs["mp"].append(m_p[:, :, 0])
            outs["cs"].append(c_s); outs["ns"].append(n_s); outs["ms"].append(m_s[:, :, 0])
            w_out = mlstm_w_out[l].astype(BF16)
        w_up, w_down = mlp_w_up[i].astype(BF16), mlp_w_down[i].astype(BF16)
        xp = _mix_out_mlp(xp, ap, w_out, norm_mlp[i], w_up, w_down, tm_p)
        xs = _mix_out_mlp(xs, a_s, w_out, norm_mlp[i], w_up, w_down, n_sample_rows)

    kv_p = (len(outs["kp"]), batch, seq, att_heads, att_dh)
    kv_s = (len(outs["ks"]), n_dec, t_dec, att_heads, att_dh)
    return (xp.reshape(batch, seq, dm), xs.reshape(n_dec, t_dec, dm),
            jnp.stack(outs["kp"]).reshape(kv_p), jnp.stack(outs["vp"]).reshape(kv_p),
            jnp.stack(outs["cp"]), jnp.stack(outs["np"]), jnp.stack(outs["mp"]),
            jnp.stack(outs["ks"]).reshape(kv_s), jnp.stack(outs["vs"]).reshape(kv_s),
            jnp.stack(outs["cs"]), jnp.stack(outs["ns"]), jnp.stack(outs["ms"]))
```
